```python
import jax, jax.numpy as jnp
from jax import lax
import numpy as np

D_MODEL = 1024
BATCH = 32
SEQ = 2048
DEPTH = 4

HEAD_DIM = 64
N_HEADS_TOTAL = D_MODEL // HEAD_DIM
N_HEADS_A = N_HEADS_TOTAL // 4
N_HEADS_B = N_HEADS_TOTAL // 4
N_HEADS_C = N_HEADS_TOTAL // 2
N_KV_C = N_HEADS_C // 4
N_IDX_HEADS = 4
IDX_DIM = HEAD_DIM
TOPK_MAX = 256
WINDOW = 128
BLOCK = 128
ROPE_THETA = 10000.0
NORM_EPS = 1e-6
MIX_WIDTH = N_HEADS_TOTAL * HEAD_DIM
D_FF = -(-8 * D_MODEL // (3 * 256)) * 256
_COLS = (N_HEADS_A * HEAD_DIM, HEAD_DIM, HEAD_DIM,
         N_IDX_HEADS * IDX_DIM, IDX_DIM, N_IDX_HEADS,
         N_HEADS_B * HEAD_DIM, N_HEADS_B * HEAD_DIM, N_HEADS_B * HEAD_DIM,
         N_HEADS_C * HEAD_DIM, N_KV_C * HEAD_DIM, N_KV_C * HEAD_DIM)
IN_COLS = sum(_COLS)

kernel_name = "hymba_style_dsa_stickbreak_swa_hybrid"


def _rms(x, g):
    xf = x.astype(jnp.float32)
    y = xf * lax.rsqrt(jnp.mean(xf * xf, axis=-1, keepdims=True) + NORM_EPS)
    return (y * g.astype(jnp.float32)).astype(x.dtype)


def _rope_tables(L):
    inv = 1.0 / (ROPE_THETA ** (jnp.arange(0, HEAD_DIM, 2, dtype=jnp.float32) / HEAD_DIM))
    ang = jnp.arange(L, dtype=jnp.float32)[:, None] * inv[None, :]
    return jnp.cos(ang), jnp.sin(ang)


def _rope(x, cos, sin):
    c = cos[None, :, None, :].astype(x.dtype)
    s = sin[None, :, None, :].astype(x.dtype)
    x1, x2 = jnp.split(x, 2, axis=-1)
    return jnp.concatenate([x1 * c - x2 * s, x2 * c + x1 * s], axis=-1)


def _blocks(a):
    B, L = a.shape[:2]
    return jnp.moveaxis(a.reshape((B, L // BLOCK, BLOCK) + a.shape[2:]), 1, 0)


def _unblocks(a):
    nb, B = a.shape[:2]
    return jnp.moveaxis(a, 0, 1).reshape((B, nb * BLOCK) + a.shape[3:])


def _split_points():
    pts, acc = [], 0
    for w in _COLS[:-1]:
        acc += w
        pts.append(acc)
    return pts


def _dsa(q, k, v, qi, ki, wi):
    B, L, H, D = q.shape
    n_sel = min(TOPK_MAX, L // 4)
    scale = D ** -0.5
    idx_scale = (N_IDX_HEADS * IDX_DIM) ** -0.5
    kpos = jnp.arange(L)

    def one_block(inp):
        qn, qin, win, n = inp
        tq = n * BLOCK + jnp.arange(BLOCK)
        rel = jax.nn.relu(jnp.einsum('bqhd,bsd->bqhs', qin, ki).astype(jnp.float32))
        score = jnp.einsum('bqh,bqhs->bqs', win.astype(jnp.float32), rel) * idx_scale
        causal = kpos[None, :] <= tq[:, None]
        score = jnp.where(causal[None], score, -jnp.inf)
        _, sel = lax.top_k(score, n_sel)
        valid = sel <= tq[None, :, None]
        ks = jax.vmap(lambda a, i: a[i])(k, sel)
        vs = jax.vmap(lambda a, i: a[i])(v, sel)
        s = jnp.einsum('bqhd,bqkd->bhqk', qn, ks).astype(jnp.float32) * scale
        s = jnp.where(valid[:, None], s, -jnp.inf)
        p = jax.nn.softmax(s, axis=-1)
        return jnp.einsum('bhqk,bqkd->bqhd', p.astype(vs.dtype), vs)

    out = lax.map(one_block, (_blocks(q), _blocks(qi), _blocks(wi), jnp.arange(L // BLOCK)))
    return _unblocks(out)


def _stick_breaking(q, k, v):
    B, L, H, D = q.shape
    scale = D ** -0.5
    kpos = jnp.arange(L)

    def one_block(inp):
        qn, n = inp
        tq = n * BLOCK + jnp.arange(BLOCK)
        strict = (kpos[None, :] < tq[:, None])[None, None]
        z = jnp.einsum('bqhd,bshd->bhqs', qn, k).astype(jnp.float32) * scale
        log_keep = jnp.where(strict, jax.nn.log_sigmoid(-z), 0.0)
        log_between = lax.cumsum(log_keep, axis=3, reverse=True) - log_keep
        a = jnp.where(strict, jnp.exp(jax.nn.log_sigmoid(z) + log_between), 0.0)
        return jnp.einsum('bhqs,bshd->bqhd', a.astype(v.dtype), v)

    out = lax.map(one_block, (_blocks(q), jnp.arange(L // BLOCK)))
    return _unblocks(out)


def _swa_sinks(q, k, v, sinks):
    B, L, H, D = q.shape
    G = H // N_KV_C
    nb = L // BLOCK
    scale = D ** -0.5
    pad = ((0, 0), (1, 0), (0, 0), (0, 0), (0, 0))
    kb = k.reshape(B, nb, BLOCK, N_KV_C, D)
    vb = v.reshape(B, nb, BLOCK, N_KV_C, D)
    kk = jnp.concatenate([jnp.pad(kb, pad)[:, :-1], kb], axis=2)
    vv = jnp.concatenate([jnp.pad(vb, pad)[:, :-1], vb], axis=2)
    qg = q.reshape(B, nb, BLOCK, N_KV_C, G, D)
    kloc = jnp.arange(2 * BLOCK)
    diff = (jnp.arange(BLOCK)[:, None] + BLOCK) - kloc[None, :]
    band = (diff >= 0) & (diff < WINDOW)
    sink = sinks.astype(jnp.float32).reshape(N_KV_C, G)[None, :, :, None, None]

    def one_block(inp):
        qn, kn, vn, n = inp
        s = jnp.einsum('bqkgd,bskd->bkgqs', qn, kn).astype(jnp.float32) * scale
        mask = band & ((n > 0) | (kloc >= BLOCK))[None, :]
        s = jnp.where(mask, s, -jnp.inf)
        m = jnp.maximum(jnp.max(s, axis=-1, keepdims=True), sink)
        e = jnp.exp(s - m)
        p = e / (jnp.sum(e, axis=-1, keepdims=True) + jnp.exp(sink - m))
        return jnp.einsum('bkgqs,bskd->bqkgd', p.astype(vn.dtype), vn)

    out = lax.map(one_block, (jnp.moveaxis(qg, 1, 0), jnp.moveaxis(kk, 1, 0),
                              jnp.moveaxis(vv, 1, 0), jnp.arange(nb)))
    return _unblocks(out).reshape(B, L, H, D)


def _mixing(h, w_in, qn_a, kn_a, qn_c, kn_c, sinks, g_out, w_o, cos, sin):
    B, L, _ = h.shape
    p = h @ w_in
    qa, ka, va, qi, ki, wi, qb, kb, vb, qc, kc, vc = jnp.split(p, _split_points(), axis=-1)
    qa = _rope(_rms(qa.reshape(B, L, N_HEADS_A, HEAD_DIM), qn_a), cos, sin)
    ka = _rope(_rms(ka.reshape(B, L, 1, HEAD_DIM), kn_a), cos, sin)[:, :, 0]
    qi = _rope(qi.reshape(B, L, N_IDX_HEADS, IDX_DIM), cos, sin)
    ki = _rope(ki.reshape(B, L, 1, IDX_DIM), cos, sin)[:, :, 0]
    oa = _dsa(qa, ka, va, qi, ki, wi)
    ob = _stick_breaking(qb.reshape(B, L, N_HEADS_B, HEAD_DIM),
                         kb.reshape(B, L, N_HEADS_B, HEAD_DIM),
                         vb.reshape(B, L, N_HEADS_B, HEAD_DIM))
    qc = _rope(_rms(qc.reshape(B, L, N_HEADS_C, HEAD_DIM), qn_c), cos, sin)
    kc = _rope(_rms(kc.reshape(B, L, N_KV_C, HEAD_DIM), kn_c), cos, sin)
    oc = _swa_sinks(qc, kc, vc.reshape(B, L, N_KV_C, HEAD_DIM), sinks)
    o = jnp.concatenate([oa, ob, oc], axis=2)
    o = _rms(o, g_out).reshape(B, L, MIX_WIDTH)
    return o @ w_o


def _swiglu(h, w_gate, w_up, w_down):
    return (jax.nn.silu(h @ w_gate) * (h @ w_up)) @ w_down


def setup_inputs(seed: int = 0) -> dict:
    key = jax.random.key(seed)
    ks = jax.random.split(key, 20)
    f = jnp.float32
    D = D_MODEL

    def nrm(k, shape, scale):
        return jax.random.normal(k, shape, f) * scale

    def gain(k, shape):
        return 1.0 + 0.02 * jax.random.normal(k, shape, f)

    return {
        "x": nrm(ks[0], (BATCH, SEQ, D), 1.0),
        "c": nrm(ks[1], (BATCH, D), 1.0),
        "ln1": gain(ks[2], (DEPTH, D)),
        "ln2": gain(ks[3], (DEPTH, D)),
        "w_mod": nrm(ks[4], (DEPTH, D, 6 * D), 0.5 * D ** -0.5),
        "b_mod": nrm(ks[5], (DEPTH, 6 * D), 0.02),
        "w_in": nrm(ks[6], (DEPTH, D, IN_COLS), D ** -0.5),
        "qn_a": gain(ks[7], (DEPTH, HEAD_DIM)),
        "kn_a": gain(ks[8], (DEPTH, HEAD_DIM)),
        "qn_c": gain(ks[9], (DEPTH, HEAD_DIM)),
        "kn_c": gain(ks[10], (DEPTH, HEAD_DIM)),
        "sinks": nrm(ks[11], (DEPTH, N_HEADS_C), 0.5),
        "g_out": gain(ks[12], (DEPTH, N_HEADS_TOTAL, HEAD_DIM)),
        "w_o": nrm(ks[13], (DEPTH, MIX_WIDTH, D), MIX_WIDTH ** -0.5),
        "w_gate": nrm(ks[14], (DEPTH, D, D_FF), D ** -0.5),
        "w_up": nrm(ks[15], (DEPTH, D, D_FF), D ** -0.5),
        "w_down": nrm(ks[16], (DEPTH, D_FF, D), D_FF ** -0.5),
    }


def reference(x, c, ln1, ln2, w_mod, b_mod, w_in, qn_a, kn_a, qn_c, kn_c, sinks, g_out, w_o,
              w_gate, w_up, w_down):
    L = x.shape[1]
    cos, sin = _rope_tables(L)
    c_act = jax.nn.silu(c)
    for l in range(DEPTH):
        mod = c_act @ w_mod[l] + b_mod[l]
        sh1, sc1, g1, sh2, sc2, g2 = [m[:, None, :] for m in jnp.split(mod, 6, axis=-1)]
        h = _rms(x, ln1[l]) * (1 + sc1) + sh1
        x = x + g1 * _mixing(h, w_in[l], qn_a[l], kn_a[l], qn_c[l], kn_c[l], sinks[l],
                             g_out[l], w_o[l], cos, sin)
        h = _rms(x, ln2[l]) * (1 + sc2) + sh2
        x = x + g2 * _swiglu(h, w_gate[l], w_up[l], w_down[l])
    return x
```

```python
import functools

import jax
import jax.numpy as jnp
from jax import lax
from jax.experimental import pallas as pl
from jax.experimental.pallas import tpu as pltpu

F32 = jnp.float32
BF16 = jnp.bfloat16
I32 = jnp.int32

HEAD_DIM = 64
N_HEADS_A = 4
N_HEADS_B = 4
N_HEADS_C = 8
N_KV_C = 2
N_IDX_HEADS = 4
TOPK_MAX = 256
WINDOW = 128
ROPE_THETA = 10000.0
NORM_EPS = 1e-6
LANES = 128
ATT_TILE = 256
TOK_TILE = 512
VMEM_LIMIT = 48 * 1024 * 1024

NEG_BIG = -1e30
INT_MIN = -2 ** 31
INT_MAX = 2 ** 31 - 1
KEY_NEG_INF = INT_MIN + 0x7FFFFF
IDX_BIG = 2 ** 30

R_QA, R_QI, R_QB, R_QC = 0, 256, 512, 768
R_VA, R_VB, R_VC, R_W = 1280, 1344, 1600, 1728
R_TOTAL = 1744
C_KAKI, C_KB, C_KC = 0, 128, 384
C_TOTAL = 512


def _cparams(sem):
    return pltpu.CompilerParams(dimension_semantics=sem, vmem_limit_bytes=VMEM_LIMIT)


def _mod_kernel(c_ref, w_ref, b_ref, o_ref):
    c = c_ref[...]
    ca = (c * jax.nn.sigmoid(c)).astype(BF16)
    y = jnp.dot(ca, w_ref[0].astype(BF16), preferred_element_type=F32)
    o_ref[0, 0] = y + b_ref[0, 0]


def _modulation(c, w_mod, b_mod):
    depth, d, _ = w_mod.shape
    b = c.shape[0]
    out = pl.pallas_call(
        _mod_kernel,
        grid=(depth, 6),
        in_specs=[
            pl.BlockSpec((b, d), lambda l, k: (0, 0)),
            pl.BlockSpec((1, d, d), lambda l, k: (l, 0, k)),
            pl.BlockSpec((1, 1, 1, d), lambda l, k: (l, k, 0, 0)),
        ],
        out_specs=pl.BlockSpec((1, 1, b, d), lambda l, k: (l, k, 0, 0)),
        out_shape=jax.ShapeDtypeStruct((depth, 6, b, d), F32),
        compiler_params=_cparams(("arbitrary", "arbitrary")),
        name="modulation",
    )(c, w_mod, b_mod.reshape(depth, 6, 1, d))
    return jnp.transpose(out, (0, 2, 1, 3))


def _inproj_kernel(x_ref, mod_ref, ln_ref, wt_ref, wn_ref, cost_ref, sint_ref, cosn_ref, sinn_ref,
                   gqa_ref, gqc_ref, gka_ref, gkc_ref,
                   qa_ref, qi_ref, qb_ref, qc_ref, w_ref, vat_ref, vbt_ref, vct_ref,
                   kaki_ref, kb_ref, kc_ref, *, tile):
    T = tile
    x = x_ref[0]
    tm = x.shape[0]
    nc = tm // T
    ms = jnp.mean(x * x, axis=-1, keepdims=True)
    h = x * lax.rsqrt(ms + NORM_EPS) * ln_ref[...]
    h = h * (1.0 + mod_ref[0, 1:2, :]) + mod_ref[0, 0:1, :]
    hb = h.astype(BF16)
    pt = lax.dot_general(wt_ref[...], hb, (((1,), (1,)), ((), ())),
                         preferred_element_type=F32)
    pn = jnp.dot(hb, wn_ref[...], preferred_element_type=F32)

    cos_t = cost_ref[...]
    sin_t = sint_ref[...]

    def rope_t(y):
        y1, y2 = y[0:32], y[32:64]
        return jnp.concatenate([y1 * cos_t - y2 * sin_t, y2 * cos_t + y1 * sin_t], axis=0)

    def rms_t(y, g):
        m = jnp.mean(y * y, axis=0, keepdims=True)
        return y * lax.rsqrt(m + NORM_EPS) * g

    zero_half = jnp.zeros((HEAD_DIM, T), BF16)

    def put_padded(ref, lead, y, half, col):
        yb = y.astype(BF16)
        for c in range(nc):
            idx = (0, c) + lead
            ref[idx + (slice(64 * half, 64 * half + 64), slice(col * T, (col + 1) * T))] = \
                yb[:, c * T:(c + 1) * T]
            ref[idx + (slice(64 * (1 - half), 64 * (1 - half) + 64), slice(col * T, (col + 1) * T))] = \
                zero_half

    gqa = gqa_ref[...]
    gqc = gqc_ref[...]
    for hh in range(N_HEADS_A):
        y = pt[R_QA + 64 * hh:R_QA + 64 * hh + 64]
        put_padded(qa_ref, (), rope_t(rms_t(y, gqa)), 0, hh)
    for hh in range(N_IDX_HEADS):
        y = pt[R_QI + 64 * hh:R_QI + 64 * hh + 64]
        put_padded(qi_ref, (), rope_t(y), 1, hh)
    for hh in range(N_HEADS_B):
        y = pt[R_QB + 64 * hh:R_QB + 64 * hh + 64]
        put_padded(qb_ref, (), y, hh % 2, hh)
    for hh in range(N_HEADS_C):
        y = pt[R_QC + 64 * hh:R_QC + 64 * hh + 64]
        g = hh // (N_HEADS_C // N_KV_C)
        put_padded(qc_ref, (g,), rope_t(rms_t(y, gqc)), g, hh % (N_HEADS_C // N_KV_C))
    for c in range(nc):
        vat_ref[0, c] = pt[R_VA:R_VA + 64, c * T:(c + 1) * T].astype(BF16)
        vbt_ref[0, c] = pt[R_VB:R_VB + 256, c * T:(c + 1) * T].astype(BF16)
        for hh in range(N_IDX_HEADS):
            w_ref[0, c, :, hh * T:(hh + 1) * T] = pt[R_W + hh:R_W + hh + 1, c * T:(c + 1) * T]
    for c in range(tm // LANES):
        vct_ref[0, c] = pt[R_VC:R_VC + 128, c * LANES:(c + 1) * LANES].astype(BF16)

    lane = lax.broadcasted_iota(I32, (tm, LANES), 1)
    left = lane < HEAD_DIM
    first = (lane & 32) == 0
    cos_n = cosn_ref[...]
    sin_n = sinn_ref[...]

    def rope_n(y):
        rot = jnp.where(first, pltpu.roll(y, 96, 1), pltpu.roll(y, 32, 1))
        return y * cos_n + rot * sin_n

    def head_rsqrt(y):
        sq = y * y
        s0 = jnp.sum(jnp.where(left, sq, 0.0), axis=-1, keepdims=True)
        s1 = jnp.sum(jnp.where(left, 0.0, sq), axis=-1, keepdims=True)
        return lax.rsqrt(jnp.where(left, s0, s1) * (1.0 / HEAD_DIM) + NORM_EPS)

    y = pn[:, C_KAKI:C_KAKI + 128]
    fac = jnp.where(left, head_rsqrt(y), 1.0) * gka_ref[...]
    kaki_ref[0] = rope_n(y * fac).astype(BF16)
    kb_ref[0] = pn[:, C_KB:C_KB + 256].astype(BF16)
    y = pn[:, C_KC:C_KC + 128]
    kc_ref[0] = rope_n(y * head_rsqrt(y) * gkc_ref[...]).astype(BF16)


def _inproj(x, mod_l, ln, wt, wn, tabs, gains):
    b, l, d = x.shape
    T = ATT_TILE
    tm = TOK_TILE
    nq = l // T
    nc = tm // T
    cos_t, sin_t, cos_n, sin_n = tabs
    gqa, gqc, gka, gkc = gains
    const2 = lambda bb, i: (0, 0)
    out_shape = (
        jax.ShapeDtypeStruct((b, nq, 128, 4 * T), BF16),
        jax.ShapeDtypeStruct((b, nq, 128, 4 * T), BF16),
        jax.ShapeDtypeStruct((b, nq, 128, 4 * T), BF16),
        jax.ShapeDtypeStruct((b, nq, 2, 128, 4 * T), BF16),
        jax.ShapeDtypeStruct((b, nq, 1, 4 * T), F32),
        jax.ShapeDtypeStruct((b, nq, 64, T), BF16),
        jax.ShapeDtypeStruct((b, nq, 256, T), BF16),
        jax.ShapeDtypeStruct((b, l // LANES, 128, LANES), BF16),
        jax.ShapeDtypeStruct((b, l, 128), BF16),
        jax.ShapeDtypeStruct((b, l, 256), BF16),
        jax.ShapeDtypeStruct((b, l, 128), BF16),
    )
    out_specs = (
        pl.BlockSpec((1, nc, 128, 4 * T), lambda bb, i: (bb, i, 0, 0)),
        pl.BlockSpec((1, nc, 128, 4 * T), lambda bb, i: (bb, i, 0, 0)),
        pl.BlockSpec((1, nc, 128, 4 * T), lambda bb, i: (bb, i, 0, 0)),
        pl.BlockSpec((1, nc, 2, 128, 4 * T), lambda bb, i: (bb, i, 0, 0, 0)),
        pl.BlockSpec((1, nc, 1, 4 * T), lambda bb, i: (bb, i, 0, 0)),
        pl.BlockSpec((1, nc, 64, T), lambda bb, i: (bb, i, 0, 0)),
        pl.BlockSpec((1, nc, 256, T), lambda bb, i: (bb, i, 0, 0)),
        pl.BlockSpec((1, tm // LANES, 128, LANES), lambda bb, i: (bb, i, 0, 0)),
        pl.BlockSpec((1, tm, 128), lambda bb, i: (bb, i, 0)),
        pl.BlockSpec((1, tm, 256), lambda bb, i: (bb, i, 0)),
        pl.BlockSpec((1, tm, 128), lambda bb, i: (bb, i, 0)),
    )
    in_specs = [
        pl.BlockSpec((1, tm, d), lambda bb, i: (bb, i, 0)),
        pl.BlockSpec((1, 6, d), lambda bb, i: (bb, 0, 0)),
        pl.BlockSpec((1, d), const2),
        pl.BlockSpec((R_TOTAL, d), const2),
        pl.BlockSpec((d, C_TOTAL), const2),
        pl.BlockSpec((32, tm), lambda bb, i: (0, i)),
        pl.BlockSpec((32, tm), lambda bb, i: (0, i)),
        pl.BlockSpec((tm, LANES), lambda bb, i: (i, 0)),
        pl.BlockSpec((tm, LANES), lambda bb, i: (i, 0)),
        pl.BlockSpec((HEAD_DIM, 1), const2),
        pl.BlockSpec((HEAD_DIM, 1), const2),
        pl.BlockSpec((1, LANES), const2),
        pl.BlockSpec((1, LANES), const2),
    ]
    return pl.pallas_call(
        functools.partial(_inproj_kernel, tile=T),
        grid=(b, l // tm),
        in_specs=in_specs,
        out_specs=out_specs,
        out_shape=out_shape,
        compiler_params=_cparams(("parallel", "arbitrary")),
        name="inproj",
    )(x, mod_l, ln, wt, wn, cos_t, sin_t, cos_n, sin_n, gqa, gqc, gka, gkc)


def _store_heads(o_ref, g_ref, heads_t):
    for p in range(len(heads_t) // 2):
        parts = []
        for o in heads_t[2 * p:2 * p + 2]:
            ms = jnp.mean(o * o, axis=0, keepdims=True)
            parts.append(o * lax.rsqrt(ms + NORM_EPS))
        ot = jnp.concatenate(parts, axis=0).T
        o_ref[0, :, 128 * p:128 * p + 128] = (ot * g_ref[:, 128 * p:128 * p + 128]).astype(BF16)


def _dsa_kernel(qi_ref, qa_ref, w_ref, kaki_ref, vat_ref, g_ref, o_ref,
                key_scr, m_scr, l_scr, acc_scr, *, tile, topk):
    T = tile
    K = topk
    i = pl.program_id(1)
    nk = i + 1
    rowk = lax.broadcasted_iota(I32, (T, T), 0)
    colq = lax.broadcasted_iota(I32, (T, T), 1)

    def key_chunk(j):
        return kaki_ref[0, pl.ds(pl.multiple_of(j * T, T), T), :]

    def score_chunk(j, diag):
        kc = key_chunk(j)
        sc = None
        for h in range(N_IDX_HEADS):
            r = jnp.dot(kc, qi_ref[0, 0, :, h * T:(h + 1) * T], preferred_element_type=F32)
            r = jnp.maximum(r, 0.0) * w_ref[0, 0, :, h * T:(h + 1) * T]
            sc = r if sc is None else sc + r
        sc = jnp.where(sc == 0.0, 0.0, sc)
        if diag:
            sc = jnp.where(rowk <= colq, sc, -jnp.inf)
        kb = pltpu.bitcast(sc, I32)
        key_scr[j] = kb ^ ((kb >> 31) & INT_MAX)

    def score_body(j, carry):
        score_chunk(j, False)
        return carry

    lax.fori_loop(0, i, score_body, 0)
    score_chunk(i, True)

    def count(pred):
        def body(j, acc):
            m = jnp.where(pred(key_scr[j], j), 1, 0).astype(I32)
            return acc + jnp.sum(m.reshape(T // 8, 8, T), axis=0)
        acc = lax.fori_loop(0, nk, body, jnp.zeros((8, T), I32))
        return jnp.sum(acc, axis=0, keepdims=True)

    tq = i * T + lax.broadcasted_iota(I32, (1, T), 1)
    small = tq < K
    lo0 = jnp.where(small, KEY_NEG_INF, INT_MIN)
    hi0 = jnp.where(small, KEY_NEG_INF + 1, INT_MAX)
    clo0 = jnp.full((1, T), K + 1, I32)
    chi0 = jnp.zeros((1, T), I32)
    nd0 = jnp.sum(jnp.where(small, 0.0, 1.0))

    def bis_body(st):
        _, lo, hi, clo, chi = st
        mid = (lo >> 1) + (hi >> 1) + (lo & hi & 1)
        c = count(lambda k, j: k >= mid)
        ge = c >= K
        lo = jnp.where(ge, mid, lo)
        clo = jnp.where(ge, c, clo)
        hi = jnp.where(ge, hi, mid)
        chi = jnp.where(ge, chi, c)
        done = (clo == K) | ((hi - 1) == lo)
        return jnp.sum(jnp.where(done, 0.0, 1.0)), lo, hi, clo, chi

    _, lo, hi, clo, chi = lax.while_loop(lambda st: st[0] > 0.0, bis_body,
                                         (nd0, lo0, hi0, clo0, chi0))

    tie = jnp.where(tq >= K, jnp.where(clo > K, 1.0, 0.0), 0.0)
    thr = jnp.maximum(lo, KEY_NEG_INF + 1)
    need = K - chi

    def tie_body(st):
        _, ilo, ihi = st
        mid = (ilo + ihi) >> 1
        c = count(lambda k, j: (k == thr) & ((rowk + j * T) < mid))
        ge = c >= need
        ihi = jnp.where(ge, mid, ihi)
        ilo = jnp.where(ge, ilo, mid)
        return jnp.sum(jnp.where((ihi - ilo) > 1, tie, 0.0)), ilo, ihi

    _, _, ihi = lax.while_loop(lambda st: st[0] > 0.0, tie_body,
                               (jnp.sum(tie), jnp.zeros((1, T), I32),
                                jnp.zeros((1, T), I32) + nk * T))
    ib = jnp.where(tie > 0.0, ihi, IDX_BIG)

    m_scr[...] = jnp.full(m_scr.shape, NEG_BIG, F32)
    l_scr[...] = jnp.zeros(l_scr.shape, F32)
    acc_scr[...] = jnp.zeros(acc_scr.shape, F32)

    def attn_body(j, carry):
        kc = key_chunk(j)
        t_el = jnp.where((rowk + j * T) < ib, thr, thr + 1)
        sel = key_scr[j] >= t_el
        vt = vat_ref[0, j]
        for h in range(N_HEADS_A):
            s = jnp.dot(kc, qa_ref[0, 0, :, h * T:(h + 1) * T], preferred_element_type=F32)
            s = jnp.where(sel, s, NEG_BIG)
            m_old = m_scr[h]
            m_new = jnp.maximum(m_old, jnp.max(s, axis=0, keepdims=True))
            p = jnp.exp(s - m_new)
            alpha = jnp.exp(m_old - m_new)
            l_scr[h] = alpha * l_scr[h] + jnp.sum(p, axis=0, keepdims=True)
            acc_scr[h] = alpha * acc_scr[h] + jnp.dot(vt, p.astype(BF16),
                                                      preferred_element_type=F32)
            m_scr[h] = m_new
        return carry

    lax.fori_loop(0, nk, attn_body, 0)
    _store_heads(o_ref, g_ref, [acc_scr[h] / l_scr[h] for h in range(N_HEADS_A)])


def _dsa(qi, qa, w, kaki, vat, g, topk):
    b, nq, _, _ = qa.shape
    T = ATT_TILE
    l = nq * T
    return pl.pallas_call(
        functools.partial(_dsa_kernel, tile=T, topk=topk),
        grid=(b, nq),
        in_specs=[
            pl.BlockSpec((1, 1, 128, 4 * T), lambda bb, i: (bb, i, 0, 0)),
            pl.BlockSpec((1, 1, 128, 4 * T), lambda bb, i: (bb, i, 0, 0)),
            pl.BlockSpec((1, 1, 1, 4 * T), lambda bb, i: (bb, i, 0, 0)),
            pl.BlockSpec((1, l, 128), lambda bb, i: (bb, 0, 0)),
            pl.BlockSpec((1, nq, 64, T), lambda bb, i: (bb, 0, 0, 0)),
            pl.BlockSpec((1, 256), lambda bb, i: (0, 0)),
        ],
        out_specs=pl.BlockSpec((1, T, 256), lambda bb, i: (bb, i, 0)),
        out_shape=jax.ShapeDtypeStruct((b, l, 256), BF16),
        scratch_shapes=[
            pltpu.VMEM((nq, T, T), I32),
            pltpu.VMEM((N_HEADS_A, 1, T), F32),
            pltpu.VMEM((N_HEADS_A, 1, T), F32),
            pltpu.VMEM((N_HEADS_A, HEAD_DIM, T), F32),
        ],
        compiler_params=_cparams(("parallel", "arbitrary")),
        name="dsa",
    )(qi, qa, w, kaki, vat, g)


def _sb_kernel(qb_ref, kb_ref, vbt_ref, g_ref, o_ref, run_scr, acc_scr, *, tile):
    T = tile
    i = pl.program_id(1)
    rowk = lax.broadcasted_iota(I32, (T, T), 0)
    colq = lax.broadcasted_iota(I32, (T, T), 1)
    strict = rowk < colq
    tmat = jnp.where(colq > rowk, 1.0, 0.0).astype(BF16)
    run_scr[...] = jnp.zeros(run_scr.shape, F32)
    acc_scr[...] = jnp.zeros(acc_scr.shape, F32)

    def chunk(j, diag):
        row0 = pl.multiple_of(j * T, T)
        for h in range(N_HEADS_B):
            p = h // 2
            kc = kb_ref[0, pl.ds(row0, T), 128 * p:128 * p + 128]
            z = jnp.dot(kc, qb_ref[0, 0, :, h * T:(h + 1) * T], preferred_element_type=F32)
            lk = -(jnp.maximum(z, 0.0) + jnp.log(1.0 + jnp.exp(-jnp.abs(z))))
            if diag:
                lk = jnp.where(strict, lk, 0.0)
            hi = lk.astype(BF16)
            lo = (lk - hi.astype(F32)).astype(BF16)
            cum = (jnp.dot(tmat, hi, preferred_element_type=F32)
                   + jnp.dot(tmat, lo, preferred_element_type=F32))
            a = jnp.exp(z + lk + cum + run_scr[h])
            if diag:
                a = jnp.where(strict, a, 0.0)
            acc_scr[h] = acc_scr[h] + jnp.dot(vbt_ref[0, j, 64 * h:64 * h + 64, :], a.astype(BF16),
                                              preferred_element_type=F32)
            run_scr[h] = run_scr[h] + jnp.sum(lk, axis=0, keepdims=True)

    chunk(i, True)

    def body(jj, carry):
        chunk(i - 1 - jj, False)
        return carry

    lax.fori_loop(0, i, body, 0)
    _store_heads(o_ref, g_ref, [acc_scr[h] for h in range(N_HEADS_B)])


def _sb(qb, kb, vbt, g):
    b, nq, _, _ = qb.shape
    T = ATT_TILE
    l = nq * T
    return pl.pallas_call(
        functools.partial(_sb_kernel, tile=T),
        grid=(b, nq),
        in_specs=[
            pl.BlockSpec((1, 1, 128, 4 * T), lambda bb, i: (bb, i, 0, 0)),
            pl.BlockSpec((1, l, 256), lambda bb, i: (bb, 0, 0)),
            pl.BlockSpec((1, nq, 256, T), lambda bb, i: (bb, 0, 0, 0)),
            pl.BlockSpec((1, 256), lambda bb, i: (0, 0)),
        ],
        out_specs=pl.BlockSpec((1, T, 256), lambda bb, i: (bb, i, 0)),
        out_shape=jax.ShapeDtypeStruct((b, l, 256), BF16),
        scratch_shapes=[
            pltpu.VMEM((N_HEADS_B, 1, T), F32),
            pltpu.VMEM((N_HEADS_B, HEAD_DIM, T), F32),
        ],
        compiler_params=_cparams(("parallel", "arbitrary")),
        name="stickbreak",
    )(qb, kb, vbt, g)


def _swa_kernel(sink_ref, qc_ref, kc_ref, vct_ref, g_ref, o_ref, *, tile):
    T = tile
    NKEY = T + WINDOW
    i = pl.program_id(1)
    start = pl.multiple_of(jnp.maximum(i * T - WINDOW, 0), WINDOW)
    kidx = start + lax.broadcasted_iota(I32, (NKEY, T), 0)
    tq = i * T + lax.broadcasted_iota(I32, (NKEY, T), 1)
    d = tq - kidx
    band = (d >= 0) & (d < WINDOW)
    kc = kc_ref[0, pl.ds(start, NKEY), :]
    c0 = start // WINDOW
    per_kv = N_HEADS_C // N_KV_C
    outs = []
    for g in range(N_KV_C):
        for hh in range(per_kv):
            s = jnp.dot(kc, qc_ref[0, 0, g, :, hh * T:(hh + 1) * T], preferred_element_type=F32)
            s = jnp.where(band, s, NEG_BIG)
            sink = sink_ref[g * per_kv + hh]
            m = jnp.maximum(jnp.max(s, axis=0, keepdims=True), sink)
            e = jnp.exp(s - m)
            den = jnp.sum(e, axis=0, keepdims=True) + jnp.exp(sink - m)
            p = (e / den).astype(BF16)
            o = None
            for c in range(NKEY // WINDOW):
                part = jnp.dot(vct_ref[0, c0 + c, 64 * g:64 * g + 64, :],
                               p[c * WINDOW:(c + 1) * WINDOW, :], preferred_element_type=F32)
                o = part if o is None else o + part
            outs.append(o)
    _store_heads(o_ref, g_ref, outs)


def _swa(sinks, qc, kc, vct, g):
    b, nq = qc.shape[:2]
    T = ATT_TILE
    l = nq * T
    return pl.pallas_call(
        functools.partial(_swa_kernel, tile=T),
        grid=(b, nq),
        in_specs=[
            pl.BlockSpec(memory_space=pltpu.SMEM),
            pl.BlockSpec((1, 1, 2, 128, 4 * T), lambda bb, i: (bb, i, 0, 0, 0)),
            pl.BlockSpec((1, l, 128), lambda bb, i: (bb, 0, 0)),
            pl.BlockSpec((1, l // LANES, 128, LANES), lambda bb, i: (bb, 0, 0, 0)),
            pl.BlockSpec((1, 512), lambda bb, i: (0, 0)),
        ],
        out_specs=pl.BlockSpec((1, T, 512), lambda bb, i: (bb, i, 0)),
        out_shape=jax.ShapeDtypeStruct((b, l, 512), BF16),
        compiler_params=_cparams(("parallel", "arbitrary")),
        name="swa",
    )(sinks, qc, kc, vct, g)


def _ffn_chunks(d_ff):
    chunks, c0 = [], 0
    while c0 < d_ff:
        cw = min(1024, d_ff - c0)
        chunks.append((c0, cw))
        c0 += cw
    return chunks


def _merge_ffn_kernel(x_ref, oa_ref, ob_ref, oc_ref, mod_ref, ln_ref, woa_ref, wob_ref, woc_ref,
                      wg_ref, wu_ref, wd_ref, out_ref):
    x = x_ref[0]
    y = (jnp.dot(oa_ref[0], woa_ref[...], preferred_element_type=F32)
         + jnp.dot(ob_ref[0], wob_ref[...], preferred_element_type=F32)
         + jnp.dot(oc_ref[0], woc_ref[...], preferred_element_type=F32))
    x1 = x + mod_ref[0, 2:3, :] * y
    ms = jnp.mean(x1 * x1, axis=-1, keepdims=True)
    h = x1 * lax.rsqrt(ms + NORM_EPS) * ln_ref[...]
    hb = (h * (1.0 + mod_ref[0, 4:5, :]) + mod_ref[0, 3:4, :]).astype(BF16)
    acc = None
    for c0, cw in _ffn_chunks(wg_ref.shape[1]):
        gt = jnp.dot(hb, wg_ref[:, c0:c0 + cw], preferred_element_type=F32)
        up = jnp.dot(hb, wu_ref[:, c0:c0 + cw], preferred_element_type=F32)
        act = (gt * jax.nn.sigmoid(gt) * up).astype(BF16)
        part = jnp.dot(act, wd_ref[c0:c0 + cw, :], preferred_element_type=F32)
        acc = part if acc is None else acc + part
    out_ref[0] = x1 + mod_ref[0, 5:6, :] * acc


def _merge_ffn(x, oa, ob, oc, mod_l, ln, woa, wob, woc, wg, wu, wd):
    b, l, d = x.shape
    tm = TOK_TILE
    dff = wg.shape[1]
    const2 = lambda bb, i: (0, 0)
    once = pl.Buffered(1)
    return pl.pallas_call(
        _merge_ffn_kernel,
        grid=(b, l // tm),
        in_specs=[
            pl.BlockSpec((1, tm, d), lambda bb, i: (bb, i, 0)),
            pl.BlockSpec((1, tm, 256), lambda bb, i: (bb, i, 0)),
            pl.BlockSpec((1, tm, 256), lambda bb, i: (bb, i, 0)),
            pl.BlockSpec((1, tm, 512), lambda bb, i: (bb, i, 0)),
            pl.BlockSpec((1, 6, d), lambda bb, i: (bb, 0, 0)),
            pl.BlockSpec((1, d), const2),
            pl.BlockSpec((256, d), const2, pipeline_mode=once),
            pl.BlockSpec((256, d), const2, pipeline_mode=once),
            pl.BlockSpec((512, d), const2, pipeline_mode=once),
            pl.BlockSpec((d, dff), const2, pipeline_mode=once),
            pl.BlockSpec((d, dff), const2, pipeline_mode=once),
            pl.BlockSpec((dff, d), const2, pipeline_mode=once),
        ],
        out_specs=pl.BlockSpec((1, tm, d), lambda bb, i: (bb, i, 0)),
        out_shape=jax.ShapeDtypeStruct((b, l, d), F32),
        compiler_params=_cparams(("parallel", "arbitrary")),
        name="merge_ffn",
    )(x, oa, ob, oc, mod_l, ln, woa, wob, woc, wg, wu, wd)


def _rope_tables(l):
    inv = 1.0 / (ROPE_THETA ** (jnp.arange(0, HEAD_DIM, 2, dtype=F32) / HEAD_DIM))
    ang = jnp.arange(l, dtype=F32)[:, None] * inv[None, :]
    cos, sin = jnp.cos(ang), jnp.sin(ang)
    cos_n = jnp.tile(cos, (1, 4))
    sin_n = jnp.concatenate([-sin, sin, -sin, sin], axis=1)
    return cos.T, sin.T, cos_n, sin_n


def _split_w_in(w_in):
    widths = (256, 64, 64, 256, 64, 4, 256, 256, 256, 512, 128, 128)
    offs = [0]
    for w in widths:
        offs.append(offs[-1] + w)
    qa, ka, va, qi, ki, wi, qb, kb, vb, qc, kc, vc = [w_in[:, :, offs[k]:offs[k + 1]]
                                                      for k in range(len(widths))]
    idx_scale = float((N_IDX_HEADS * HEAD_DIM) ** -0.5)
    att_scale = float(HEAD_DIM ** -0.5)
    pad = jnp.zeros(wi.shape[:2] + (R_TOTAL - R_W - N_IDX_HEADS,), w_in.dtype)
    wt = jnp.concatenate([qa, qi, qb * att_scale, qc, va, vb, vc, wi * idx_scale, pad], axis=2)
    wt = jnp.swapaxes(wt, 1, 2).astype(BF16)
    wn = jnp.concatenate([ka, ki, kb, kc], axis=2).astype(BF16)
    return wt, wn


@jax.jit
def kernel(x, c, ln1, ln2, w_mod, b_mod, w_in, qn_a, kn_a, qn_c, kn_c, sinks, g_out, w_o,
           w_gate, w_up, w_down):
    depth = w_in.shape[0]
    b, l, d = x.shape
    assert l % TOK_TILE == 0 and l >= ATT_TILE + WINDOW
    topk = min(TOPK_MAX, l // 4)
    att_scale = float(HEAD_DIM ** -0.5)

    tabs = _rope_tables(l)
    mod = _modulation(c, w_mod, b_mod)
    wt, wn = _split_w_in(w_in)
    w_o_b = w_o.astype(BF16)
    w_g_b, w_u_b, w_d_b = w_gate.astype(BF16), w_up.astype(BF16), w_down.astype(BF16)
    ones = jnp.ones((HEAD_DIM,), F32)

    for li in range(depth):
        gains = ((qn_a[li] * att_scale).reshape(HEAD_DIM, 1),
                 (qn_c[li] * att_scale).reshape(HEAD_DIM, 1),
                 jnp.concatenate([kn_a[li], ones]).reshape(1, LANES),
                 jnp.concatenate([kn_c[li], kn_c[li]]).reshape(1, LANES))
        (qa, qi, qb, qc, w, vat, vbt, vct, kaki, kb, kc) = _inproj(
            x, mod[li], ln1[li].reshape(1, d), wt[li], wn[li], tabs, gains)
        g = g_out[li].reshape(1, -1)
        oa = _dsa(qi, qa, w, kaki, vat, g[:, 0:256], topk)
        ob = _sb(qb, kb, vbt, g[:, 256:512])
        oc = _swa(sinks[li], qc, kc, vct, g[:, 512:1024])
        x = _merge_ffn(x, oa, ob, oc, mod[li], ln2[li].reshape(1, d),
                       w_o_b[li, 0:256], w_o_b[li, 256:512], w_o_b[li, 512:1024],
                       w_g_b[li], w_u_b[li], w_d_b[li])
    return x
```

```python
import functools

import jax
import jax.numpy as jnp
from jax import lax
from jax.experimental import pallas as pl
from jax.experimental.pallas import tpu as pltpu

F32 = jnp.float32
BF16 = jnp.bfloat16
I32 = jnp.int32

HEAD_DIM = 64
N_HEADS_A = 4
N_HEADS_B = 4
N_HEADS_C = 8
N_KV_C = 2
N_IDX_HEADS = 4
TOPK_MAX = 256
WINDOW = 128
ROPE_THETA = 10000.0
NORM_EPS = 1e-6
LANES = 128
ATT_TILE = 256
TOK_TILE = 512
VMEM_LIMIT = 48 * 1024 * 1024

NEG_BIG = -1e30
INT_MIN = -2 ** 31
INT_MAX = 2 ** 31 - 1
KEY_NEG_INF = -0x7F800000
IDX_BIG = 2 ** 30
LOG2E = 1.4426950408889634
PROBES_PER_CHECK = 3
FLOAT_PROBES = 24

R_QA, R_QI, R_QB, R_QC = 0, 256, 512, 768
R_VA, R_VB, R_VC, R_W = 1280, 1344, 1600, 1728
R_TOTAL = 1744
C_KAKI, C_KB, C_KC = 0, 128, 384
C_TOTAL = 512


def _cparams(sem):
    return pltpu.CompilerParams(dimension_semantics=sem, vmem_limit_bytes=VMEM_LIMIT)


def _mod_kernel(c_ref, w_ref, b_ref, o_ref):
    c = c_ref[...]
    ca = (c * jax.nn.sigmoid(c)).astype(BF16)
    y = jnp.dot(ca, w_ref[0].astype(BF16), preferred_element_type=F32)
    o_ref[0, 0] = y + b_ref[0, 0]


def _modulation(c, w_mod, b_mod):
    depth, d, _ = w_mod.shape
    b = c.shape[0]
    out = pl.pallas_call(
        _mod_kernel,
        grid=(depth, 6),
        in_specs=[
            pl.BlockSpec((b, d), lambda l, k: (0, 0)),
            pl.BlockSpec((1, d, d), lambda l, k: (l, 0, k)),
            pl.BlockSpec((1, 1, 1, d), lambda l, k: (l, k, 0, 0)),
        ],
        out_specs=pl.BlockSpec((1, 1, b, d), lambda l, k: (l, k, 0, 0)),
        out_shape=jax.ShapeDtypeStruct((depth, 6, b, d), F32),
        compiler_params=_cparams(("arbitrary", "arbitrary")),
        name="modulation",
    )(c, w_mod, b_mod.reshape(depth, 6, 1, d))
    return jnp.transpose(out, (0, 2, 1, 3))


def _inproj_kernel(x_ref, mod_ref, ln_ref, wt_ref, wn_ref, cost_ref, sint_ref, cosn_ref, sinn_ref,
                   gqa_ref, gqc_ref, gka_ref, gkc_ref,
                   qa_ref, qi_ref, qb_ref, qc_ref, w_ref, vat_ref, vbt_ref, vct_ref,
                   kaki_ref, kb_ref, kc_ref, *, tile):
    T = tile
    x = x_ref[0]
    tm = x.shape[0]
    nc = tm // T
    ms = jnp.mean(x * x, axis=-1, keepdims=True)
    h = x * lax.rsqrt(ms + NORM_EPS) * ln_ref[...]
    h = h * (1.0 + mod_ref[0, 1:2, :]) + mod_ref[0, 0:1, :]
    hb = h.astype(BF16)
    pt = lax.dot_general(wt_ref[...], hb, (((1,), (1,)), ((), ())),
                         preferred_element_type=F32)
    pn = jnp.dot(hb, wn_ref[...], preferred_element_type=F32)

    cos_t = cost_ref[...]
    sin_t = sint_ref[...]

    def rope_t(y):
        y1, y2 = y[0:32], y[32:64]
        return jnp.concatenate([y1 * cos_t - y2 * sin_t, y2 * cos_t + y1 * sin_t], axis=0)

    def rms_t(y, g):
        m = jnp.mean(y * y, axis=0, keepdims=True)
        return y * lax.rsqrt(m + NORM_EPS) * g

    zero_half = jnp.zeros((HEAD_DIM, T), BF16)

    def put_padded(ref, lead, y, half, col):
        yb = y.astype(BF16)
        for c in range(nc):
            idx = (0, c) + lead
            ref[idx + (slice(64 * half, 64 * half + 64), slice(col * T, (col + 1) * T))] = \
                yb[:, c * T:(c + 1) * T]
            ref[idx + (slice(64 * (1 - half), 64 * (1 - half) + 64), slice(col * T, (col + 1) * T))] = \
                zero_half

    gqa = gqa_ref[...]
    gqc = gqc_ref[...]
    for hh in range(N_HEADS_A):
        y = pt[R_QA + 64 * hh:R_QA + 64 * hh + 64]
        put_padded(qa_ref, (), rope_t(rms_t(y, gqa)), 0, hh)
    for hh in range(N_IDX_HEADS):
        y = pt[R_QI + 64 * hh:R_QI + 64 * hh + 64]
        put_padded(qi_ref, (), rope_t(y), 1, hh)
    for hh in range(N_HEADS_B):
        y = pt[R_QB + 64 * hh:R_QB + 64 * hh + 64]
        put_padded(qb_ref, (), y, hh % 2, hh)
    for hh in range(N_HEADS_C):
        y = pt[R_QC + 64 * hh:R_QC + 64 * hh + 64]
        g = hh // (N_HEADS_C // N_KV_C)
        put_padded(qc_ref, (g,), rope_t(rms_t(y, gqc)), g, hh % (N_HEADS_C // N_KV_C))
    for c in range(nc):
        vat_ref[0, c] = pt[R_VA:R_VA + 64, c * T:(c + 1) * T].astype(BF16)
        vbt_ref[0, c] = pt[R_VB:R_VB + 256, c * T:(c + 1) * T].astype(BF16)
        for hh in range(N_IDX_HEADS):
            w_ref[0, c, :, hh * T:(hh + 1) * T] = pt[R_W + hh:R_W + hh + 1, c * T:(c + 1) * T]
    for c in range(tm // LANES):
        vct_ref[0, c] = pt[R_VC:R_VC + 128, c * LANES:(c + 1) * LANES].astype(BF16)

    lane = lax.broadcasted_iota(I32, (tm, LANES), 1)
    left = lane < HEAD_DIM
    first = (lane & 32) == 0
    cos_n = cosn_ref[...]
    sin_n = sinn_ref[...]

    def rope_n(y):
        rot = jnp.where(first, pltpu.roll(y, 96, 1), pltpu.roll(y, 32, 1))
        return y * cos_n + rot * sin_n

    def head_rsqrt(y):
        sq = y * y
        s0 = jnp.sum(jnp.where(left, sq, 0.0), axis=-1, keepdims=True)
        s1 = jnp.sum(jnp.where(left, 0.0, sq), axis=-1, keepdims=True)
        return lax.rsqrt(jnp.where(left, s0, s1) * (1.0 / HEAD_DIM) + NORM_EPS)

    y = pn[:, C_KAKI:C_KAKI + 128]
    fac = jnp.where(left, head_rsqrt(y), 1.0) * gka_ref[...]
    kaki_ref[0] = rope_n(y * fac).astype(BF16)
    kb_ref[0] = pn[:, C_KB:C_KB + 256].astype(BF16)
    y = pn[:, C_KC:C_KC + 128]
    kc_ref[0] = rope_n(y * head_rsqrt(y) * gkc_ref[...]).astype(BF16)


def _inproj(x, mod_l, ln, wt, wn, tabs, gains):
    b, l, d = x.shape
    T = ATT_TILE
    tm = TOK_TILE
    nq = l // T
    nc = tm // T
    cos_t, sin_t, cos_n, sin_n = tabs
    gqa, gqc, gka, gkc = gains
    const2 = lambda bb, i: (0, 0)
    out_shape = (
        jax.ShapeDtypeStruct((b, nq, 128, 4 * T), BF16),
        jax.ShapeDtypeStruct((b, nq, 128, 4 * T), BF16),
        jax.ShapeDtypeStruct((b, nq, 128, 4 * T), BF16),
        jax.ShapeDtypeStruct((b, nq, 2, 128, 4 * T), BF16),
        jax.ShapeDtypeStruct((b, nq, 1, 4 * T), F32),
        jax.ShapeDtypeStruct((b, nq, 64, T), BF16),
        jax.ShapeDtypeStruct((b, nq, 256, T), BF16),
        jax.ShapeDtypeStruct((b, l // LANES, 128, LANES), BF16),
        jax.ShapeDtypeStruct((b, l, 128), BF16),
        jax.ShapeDtypeStruct((b, l, 256), BF16),
        jax.ShapeDtypeStruct((b, l, 128), BF16),
    )
    out_specs = (
        pl.BlockSpec((1, nc, 128, 4 * T), lambda bb, i: (bb, i, 0, 0)),
        pl.BlockSpec((1, nc, 128, 4 * T), lambda bb, i: (bb, i, 0, 0)),
        pl.BlockSpec((1, nc, 128, 4 * T), lambda bb, i: (bb, i, 0, 0)),
        pl.BlockSpec((1, nc, 2, 128, 4 * T), lambda bb, i: (bb, i, 0, 0, 0)),
        pl.BlockSpec((1, nc, 1, 4 * T), lambda bb, i: (bb, i, 0, 0)),
        pl.BlockSpec((1, nc, 64, T), lambda bb, i: (bb, i, 0, 0)),
        pl.BlockSpec((1, nc, 256, T), lambda bb, i: (bb, i, 0, 0)),
        pl.BlockSpec((1, tm // LANES, 128, LANES), lambda bb, i: (bb, i, 0, 0)),
        pl.BlockSpec((1, tm, 128), lambda bb, i: (bb, i, 0)),
        pl.BlockSpec((1, tm, 256), lambda bb, i: (bb, i, 0)),
        pl.BlockSpec((1, tm, 128), lambda bb, i: (bb, i, 0)),
    )
    in_specs = [
        pl.BlockSpec((1, tm, d), lambda bb, i: (bb, i, 0)),
        pl.BlockSpec((1, 6, d), lambda bb, i: (bb, 0, 0)),
        pl.BlockSpec((1, d), const2),
        pl.BlockSpec((R_TOTAL, d), const2),
        pl.BlockSpec((d, C_TOTAL), const2),
        pl.BlockSpec((32, tm), lambda bb, i: (0, i)),
        pl.BlockSpec((32, tm), lambda bb, i: (0, i)),
        pl.BlockSpec((tm, LANES), lambda bb, i: (i, 0)),
        pl.BlockSpec((tm, LANES), lambda bb, i: (i, 0)),
        pl.BlockSpec((HEAD_DIM, 1), const2),
        pl.BlockSpec((HEAD_DIM, 1), const2),
        pl.BlockSpec((1, LANES), const2),
        pl.BlockSpec((1, LANES), const2),
    ]
    return pl.pallas_call(
        functools.partial(_inproj_kernel, tile=T),
        grid=(b, l // tm),
        in_specs=in_specs,
        out_specs=out_specs,
        out_shape=out_shape,
        compiler_params=_cparams(("parallel", "arbitrary")),
        name="inproj",
    )(x, mod_l, ln, wt, wn, cos_t, sin_t, cos_n, sin_n, gqa, gqc, gka, gkc)


def _store_heads(o_ref, g_ref, heads_t):
    for p in range(len(heads_t) // 2):
        parts = []
        for o in heads_t[2 * p:2 * p + 2]:
            ms = jnp.mean(o * o, axis=0, keepdims=True)
            parts.append(o * lax.rsqrt(ms + NORM_EPS))
        ot = jnp.concatenate(parts, axis=0).T
        o_ref[0, :, 128 * p:128 * p + 128] = (ot * g_ref[:, 128 * p:128 * p + 128]).astype(BF16)


def _float_key(x):
    kb = pltpu.bitcast(x, I32)
    return jnp.where(kb < 0, INT_MIN - kb, kb)


def _key_float(k):
    return pltpu.bitcast(jnp.where(k < 0, INT_MIN - k, k), F32)


def _fold8(x, op):
    t = x.shape[0]
    return op(x.reshape(t // 8, 8, x.shape[1]), axis=0)


def _dsa_kernel(qi_ref, qa_ref, w_ref, kaki_ref, vat_ref, g_ref, o_ref,
                key_scr, s_scr, ib_scr, m_scr, l_scr, acc_scr, *, tile, topk):
    T = tile
    K = topk
    i = pl.program_id(1)
    nk = i + 1
    rowk = lax.broadcasted_iota(I32, (T, T), 0)
    colq = lax.broadcasted_iota(I32, (T, T), 1)

    def key_chunk(j):
        return kaki_ref[0, pl.ds(pl.multiple_of(j * T, T), T), :]

    def score_chunk(j, diag, mx8, mn8):
        kc = key_chunk(j)
        sc = None
        for h in range(N_IDX_HEADS):
            r = jnp.dot(kc, qi_ref[0, 0, :, h * T:(h + 1) * T], preferred_element_type=F32)
            r = jnp.maximum(r, 0.0) * w_ref[0, 0, :, h * T:(h + 1) * T]
            sc = r if sc is None else sc + r
        key = _float_key(sc)
        if diag:
            causal = rowk <= colq
            key_hi = jnp.where(causal, key, KEY_NEG_INF)
            key_lo = jnp.where(causal, key, INT_MAX)
        else:
            key_hi = key_lo = key
        key_scr[j] = key_hi
        return (jnp.maximum(mx8, _fold8(key_hi, jnp.max)),
                jnp.minimum(mn8, _fold8(key_lo, jnp.min)))

    mx8, mn8 = lax.fori_loop(
        0, i, lambda j, c: score_chunk(j, False, *c),
        (jnp.full((8, T), INT_MIN, I32), jnp.full((8, T), INT_MAX, I32)))
    mx8, mn8 = score_chunk(i, True, mx8, mn8)
    rmax = jnp.max(mx8, axis=0, keepdims=True)
    rmin = jnp.min(mn8, axis=0, keepdims=True)

    def count_ge(mid):
        def body(j, acc):
            return acc + _fold8(jnp.where(key_scr[j] >= mid, 1, 0), jnp.sum)
        acc = lax.fori_loop(0, nk, body, jnp.zeros((8, T), I32))
        return jnp.sum(acc, axis=0, keepdims=True)

    tq = i * T + lax.broadcasted_iota(I32, (1, T), 1)
    big = tq >= K
    lo0 = jnp.where(big, rmin, KEY_NEG_INF)
    hi0 = jnp.where(big, rmax + 1, KEY_NEG_INF + 1)
    clo0 = tq + 1
    chi0 = jnp.zeros((1, T), I32)

    def probe(it, lo, hi, clo, chi):
        fmid = _float_key(0.5 * _key_float(lo) + 0.5 * _key_float(hi))
        kmid = (lo >> 1) + (hi >> 1) + (lo & hi & 1)
        inside = jnp.where(fmid > lo, jnp.where(fmid < hi, it, FLOAT_PROBES), FLOAT_PROBES)
        mid = jnp.where(inside < FLOAT_PROBES, fmid, kmid)
        mid = jnp.where(lo == 0, jnp.where(hi > 1, 1, mid), mid)
        mid = jnp.where(lo < 0, jnp.where(hi > 0, 0, mid), mid)
        c = count_ge(mid)
        ge = c >= K
        return (jnp.where(ge, mid, lo), jnp.where(ge, hi, mid),
                jnp.where(ge, c, clo), jnp.where(ge, chi, c))

    def bis_body(st):
        _, it, lo, hi, clo, chi = st
        for _ in range(PROBES_PER_CHECK):
            lo, hi, clo, chi = probe(it, lo, hi, clo, chi)
            it = it + 1
        open_ = jnp.where(clo == K, 0.0, jnp.where((hi - 1) == lo, 0.0, 1.0))
        return jnp.sum(open_), it, lo, hi, clo, chi

    _, _, lo, hi, clo, chi = lax.while_loop(
        lambda st: st[0] > 0.0, bis_body,
        (jnp.sum(jnp.where(big, 1.0, 0.0)), jnp.zeros((1, T), I32), lo0, hi0, clo0, chi0))

    tie = jnp.where(big, jnp.where(clo > K, 1.0, 0.0), 0.0)
    thr = jnp.maximum(lo, KEY_NEG_INF + 1)
    need = (K - chi).astype(F32)
    ib_scr[...] = jnp.full((1, T), IDX_BIG, I32)

    @pl.when(jnp.sum(tie) > 0.0)
    def _():
        before = jnp.where(colq < rowk, 1.0, 0.0).astype(BF16)

        def body(j, carry):
            run, ibm8 = carry
            eq = jnp.where(key_scr[j] == thr, 1.0, 0.0)
            pc = jnp.dot(before, eq.astype(BF16), preferred_element_type=F32) + run
            idx1 = (rowk + (j * T + 1)).astype(F32)
            taken = jnp.where(pc < need, eq * idx1, 0.0)
            return (run + jnp.sum(_fold8(eq, jnp.sum), axis=0, keepdims=True),
                    jnp.maximum(ibm8, _fold8(taken, jnp.max)))

        _, ibm8 = lax.fori_loop(0, nk, body, (jnp.zeros((1, T), F32), jnp.zeros((8, T), F32)))
        ibm = jnp.max(ibm8, axis=0, keepdims=True).astype(I32)
        ib_scr[...] = jnp.where(tie > 0.0, ibm, IDX_BIG)

    ib = ib_scr[...]

    m_scr[...] = jnp.full(m_scr.shape, NEG_BIG, F32)
    l_scr[...] = jnp.zeros(l_scr.shape, F32)
    acc_scr[...] = jnp.zeros(acc_scr.shape, F32)

    def attn_body(j, carry):
        kc = key_chunk(j)
        t_el = jnp.where((rowk + j * T) < ib, thr, thr + 1)
        sel = key_scr[j] >= t_el
        vt = vat_ref[0, j]
        m_new = []
        for h in range(N_HEADS_A):
            s = jnp.dot(kc, qa_ref[0, 0, :, h * T:(h + 1) * T], preferred_element_type=F32)
            s = jnp.where(sel, s, NEG_BIG)
            s_scr[h] = s
            mc = jnp.max(_fold8(s, jnp.max), axis=0, keepdims=True)
            m_new.append(jnp.maximum(m_scr[h], mc))
        for h in range(N_HEADS_A):
            p = jnp.exp2(s_scr[h] - m_new[h])
            alpha = jnp.exp2(m_scr[h] - m_new[h])
            l_scr[h] = alpha * l_scr[h] + jnp.sum(_fold8(p, jnp.sum), axis=0, keepdims=True)
            acc_scr[h] = alpha * acc_scr[h] + jnp.dot(vt, p.astype(BF16),
                                                      preferred_element_type=F32)
            m_scr[h] = m_new[h]
        return carry

    lax.fori_loop(0, nk, attn_body, 0)
    _store_heads(o_ref, g_ref, [acc_scr[h] / l_scr[h] for h in range(N_HEADS_A)])


def _dsa(qi, qa, w, kaki, vat, g, topk):
    b, nq, _, _ = qa.shape
    T = ATT_TILE
    l = nq * T
    return pl.pallas_call(
        functools.partial(_dsa_kernel, tile=T, topk=topk),
        grid=(b, nq),
        in_specs=[
            pl.BlockSpec((1, 1, 128, 4 * T), lambda bb, i: (bb, i, 0, 0)),
            pl.BlockSpec((1, 1, 128, 4 * T), lambda bb, i: (bb, i, 0, 0)),
            pl.BlockSpec((1, 1, 1, 4 * T), lambda bb, i: (bb, i, 0, 0)),
            pl.BlockSpec((1, l, 128), lambda bb, i: (bb, 0, 0)),
            pl.BlockSpec((1, nq, 64, T), lambda bb, i: (bb, 0, 0, 0)),
            pl.BlockSpec((1, 256), lambda bb, i: (0, 0)),
        ],
        out_specs=pl.BlockSpec((1, T, 256), lambda bb, i: (bb, i, 0)),
        out_shape=jax.ShapeDtypeStruct((b, l, 256), BF16),
        scratch_shapes=[
            pltpu.VMEM((nq, T, T), I32),
            pltpu.VMEM((N_HEADS_A, T, T), F32),
            pltpu.VMEM((1, T), I32),
            pltpu.VMEM((N_HEADS_A, 1, T), F32),
            pltpu.VMEM((N_HEADS_A, 1, T), F32),
            pltpu.VMEM((N_HEADS_A, HEAD_DIM, T), F32),
        ],
        compiler_params=_cparams(("parallel", "arbitrary")),
        name="dsa",
    )(qi, qa, w, kaki, vat, g)


def _sb_kernel(qb_ref, kb_ref, vbt_ref, g_ref, o_ref, run_scr, acc_scr, z_scr, c_scr, hl_scr, *,
               tile):
    T = tile
    i = pl.program_id(1)
    rowk = lax.broadcasted_iota(I32, (T, T), 0)
    colq = lax.broadcasted_iota(I32, (T, T), 1)
    strict = rowk < colq
    col2 = lax.broadcasted_iota(I32, (T, 2 * T), 1)
    row2 = lax.broadcasted_iota(I32, (T, 2 * T), 0)
    tmat2 = jnp.where((col2 & (T - 1)) > row2, 1.0, 0.0).astype(BF16)
    run_scr[...] = jnp.zeros(run_scr.shape, F32)
    acc_scr[...] = jnp.zeros(acc_scr.shape, F32)

    def chunk(j, diag):
        row0 = pl.multiple_of(j * T, T)
        for h in range(N_HEADS_B):
            p = h // 2
            kc = kb_ref[0, pl.ds(row0, T), 128 * p:128 * p + 128]
            z_scr[h] = jnp.dot(kc, qb_ref[0, 0, :, h * T:(h + 1) * T],
                               preferred_element_type=F32)
        tots = []
        for h in range(N_HEADS_B):
            z = z_scr[h]
            nz = -z
            lk = jnp.minimum(nz, 0.0) - jnp.log2(1.0 + jnp.exp2(jnp.minimum(z, nz)))
            if diag:
                lk = jnp.where(strict, lk, 0.0)
            hi = lk.astype(BF16)
            hl_scr[h, 0:T, :] = hi
            hl_scr[h, T:2 * T, :] = (lk - hi.astype(F32)).astype(BF16)
            z_scr[h] = z + lk
            tots.append(jnp.sum(_fold8(lk, jnp.sum), axis=0, keepdims=True))
            c_scr[h] = jnp.dot(tmat2, hl_scr[h], preferred_element_type=F32)
        for h in range(N_HEADS_B):
            a = jnp.exp2(z_scr[h] + c_scr[h] + run_scr[h])
            if diag:
                a = jnp.where(strict, a, 0.0)
            acc_scr[h] = acc_scr[h] + jnp.dot(vbt_ref[0, j, 64 * h:64 * h + 64, :], a.astype(BF16),
                                              preferred_element_type=F32)
            run_scr[h] = run_scr[h] + tots[h]

    chunk(i, True)

    def body(jj, carry):
        chunk(i - 1 - jj, False)
        return carry

    lax.fori_loop(0, i, body, 0)
    _store_heads(o_ref, g_ref, [acc_scr[h] for h in range(N_HEADS_B)])


def _sb(qb, kb, vbt, g):
    b, nq, _, _ = qb.shape
    T = ATT_TILE
    l = nq * T
    return pl.pallas_call(
        functools.partial(_sb_kernel, tile=T),
        grid=(b, nq),
        in_specs=[
            pl.BlockSpec((1, 1, 128, 4 * T), lambda bb, i: (bb, i, 0, 0)),
            pl.BlockSpec((1, l, 256), lambda bb, i: (bb, 0, 0)),
            pl.BlockSpec((1, nq, 256, T), lambda bb, i: (bb, 0, 0, 0)),
            pl.BlockSpec((1, 256), lambda bb, i: (0, 0)),
        ],
        out_specs=pl.BlockSpec((1, T, 256), lambda bb, i: (bb, i, 0)),
        out_shape=jax.ShapeDtypeStruct((b, l, 256), BF16),
        scratch_shapes=[
            pltpu.VMEM((N_HEADS_B, 1, T), F32),
            pltpu.VMEM((N_HEADS_B, HEAD_DIM, T), F32),
            pltpu.VMEM((N_HEADS_B, T, T), F32),
            pltpu.VMEM((N_HEADS_B, T, T), F32),
            pltpu.VMEM((N_HEADS_B, 2 * T, T), BF16),
        ],
        compiler_params=_cparams(("parallel", "arbitrary")),
        name="stickbreak",
    )(qb, kb, vbt, g)


def _swa_kernel(sink_ref, qc_ref, kc_ref, vct_ref, g_ref, o_ref, *, tile):
    T = tile
    NKEY = T + WINDOW
    i = pl.program_id(1)
    start = pl.multiple_of(jnp.maximum(i * T - WINDOW, 0), WINDOW)
    kidx = start + lax.broadcasted_iota(I32, (NKEY, T), 0)
    tq = i * T + lax.broadcasted_iota(I32, (NKEY, T), 1)
    d = tq - kidx
    band = (d >= 0) & (d < WINDOW)
    kc = kc_ref[0, pl.ds(start, NKEY), :]
    c0 = start // WINDOW
    per_kv = N_HEADS_C // N_KV_C
    outs = []
    for g in range(N_KV_C):
        for hh in range(per_kv):
            s = jnp.dot(kc, qc_ref[0, 0, g, :, hh * T:(hh + 1) * T], preferred_element_type=F32)
            s = jnp.where(band, s, NEG_BIG)
            sink = sink_ref[g * per_kv + hh]
            m = jnp.maximum(jnp.max(s, axis=0, keepdims=True), sink)
            e = jnp.exp(s - m)
            den = jnp.sum(e, axis=0, keepdims=True) + jnp.exp(sink - m)
            p = (e / den).astype(BF16)
            o = None
            for c in range(NKEY // WINDOW):
                part = jnp.dot(vct_ref[0, c0 + c, 64 * g:64 * g + 64, :],
                               p[c * WINDOW:(c + 1) * WINDOW, :], preferred_element_type=F32)
                o = part if o is None else o + part
            outs.append(o)
    _store_heads(o_ref, g_ref, outs)


def _swa(sinks, qc, kc, vct, g):
    b, nq = qc.shape[:2]
    T = ATT_TILE
    l = nq * T
    return pl.pallas_call(
        functools.partial(_swa_kernel, tile=T),
        grid=(b, nq),
        in_specs=[
            pl.BlockSpec(memory_space=pltpu.SMEM),
            pl.BlockSpec((1, 1, 2, 128, 4 * T), lambda bb, i: (bb, i, 0, 0, 0)),
            pl.BlockSpec((1, l, 128), lambda bb, i: (bb, 0, 0)),
            pl.BlockSpec((1, l // LANES, 128, LANES), lambda bb, i: (bb, 0, 0, 0)),
            pl.BlockSpec((1, 512), lambda bb, i: (0, 0)),
        ],
        out_specs=pl.BlockSpec((1, T, 512), lambda bb, i: (bb, i, 0)),
        out_shape=jax.ShapeDtypeStruct((b, l, 512), BF16),
        compiler_params=_cparams(("parallel", "arbitrary")),
        name="swa",
    )(sinks, qc, kc, vct, g)


def _ffn_chunks(d_ff):
    chunks, c0 = [], 0
    while c0 < d_ff:
        cw = min(1024, d_ff - c0)
        chunks.append((c0, cw))
        c0 += cw
    return chunks


def _merge_ffn_kernel(x_ref, oa_ref, ob_ref, oc_ref, mod_ref, ln_ref, woa_ref, wob_ref, woc_ref,
                      wg_ref, wu_ref, wd_ref, out_ref):
    x = x_ref[0]
    y = (jnp.dot(oa_ref[0], woa_ref[...], preferred_element_type=F32)
         + jnp.dot(ob_ref[0], wob_ref[...], preferred_element_type=F32)
         + jnp.dot(oc_ref[0], woc_ref[...], preferred_element_type=F32))
    x1 = x + mod_ref[0, 2:3, :] * y
    ms = jnp.mean(x1 * x1, axis=-1, keepdims=True)
    h = x1 * lax.rsqrt(ms + NORM_EPS) * ln_ref[...]
    hb = (h * (1.0 + mod_ref[0, 4:5, :]) + mod_ref[0, 3:4, :]).astype(BF16)
    acc = None
    for c0, cw in _ffn_chunks(wg_ref.shape[1]):
        gt = jnp.dot(hb, wg_ref[:, c0:c0 + cw], preferred_element_type=F32)
        up = jnp.dot(hb, wu_ref[:, c0:c0 + cw], preferred_element_type=F32)
        act = (gt * jax.nn.sigmoid(gt) * up).astype(BF16)
        part = jnp.dot(act, wd_ref[c0:c0 + cw, :], preferred_element_type=F32)
        acc = part if acc is None else acc + part
    out_ref[0] = x1 + mod_ref[0, 5:6, :] * acc


def _merge_ffn(x, oa, ob, oc, mod_l, ln, woa, wob, woc, wg, wu, wd):
    b, l, d = x.shape
    tm = TOK_TILE
    dff = wg.shape[1]
    const2 = lambda bb, i: (0, 0)
    once = pl.Buffered(1)
    return pl.pallas_call(
        _merge_ffn_kernel,
        grid=(b, l // tm),
        in_specs=[
            pl.BlockSpec((1, tm, d), lambda bb, i: (bb, i, 0)),
            pl.BlockSpec((1, tm, 256), lambda bb, i: (bb, i, 0)),
            pl.BlockSpec((1, tm, 256), lambda bb, i: (bb, i, 0)),
            pl.BlockSpec((1, tm, 512), lambda bb, i: (bb, i, 0)),
            pl.BlockSpec((1, 6, d), lambda bb, i: (bb, 0, 0)),
            pl.BlockSpec((1, d), const2),
            pl.BlockSpec((256, d), const2, pipeline_mode=once),
            pl.BlockSpec((256, d), const2, pipeline_mode=once),
            pl.BlockSpec((512, d), const2, pipeline_mode=once),
            pl.BlockSpec((d, dff), const2, pipeline_mode=once),
            pl.BlockSpec((d, dff), const2, pipeline_mode=once),
            pl.BlockSpec((dff, d), const2, pipeline_mode=once),
        ],
        out_specs=pl.BlockSpec((1, tm, d), lambda bb, i: (bb, i, 0)),
        out_shape=jax.ShapeDtypeStruct((b, l, d), F32),
        compiler_params=_cparams(("parallel", "arbitrary")),
        name="merge_ffn",
    )(x, oa, ob, oc, mod_l, ln, woa, wob, woc, wg, wu, wd)


def _rope_tables(l):
    inv = 1.0 / (ROPE_THETA ** (jnp.arange(0, HEAD_DIM, 2, dtype=F32) / HEAD_DIM))
    ang = jnp.arange(l, dtype=F32)[:, None] * inv[None, :]
    cos, sin = jnp.cos(ang), jnp.sin(ang)
    cos_n = jnp.tile(cos, (1, 4))
    sin_n = jnp.concatenate([-sin, sin, -sin, sin], axis=1)
    return cos.T, sin.T, cos_n, sin_n


def _split_w_in(w_in):
    widths = (256, 64, 64, 256, 64, 4, 256, 256, 256, 512, 128, 128)
    offs = [0]
    for w in widths:
        offs.append(offs[-1] + w)
    qa, ka, va, qi, ki, wi, qb, kb, vb, qc, kc, vc = [w_in[:, :, offs[k]:offs[k + 1]]
                                                      for k in range(len(widths))]
    idx_scale = float((N_IDX_HEADS * HEAD_DIM) ** -0.5)
    att_scale = float(HEAD_DIM ** -0.5)
    pad = jnp.zeros(wi.shape[:2] + (R_TOTAL - R_W - N_IDX_HEADS,), w_in.dtype)
    wt = jnp.concatenate([qa, qi, qb * (att_scale * LOG2E), qc, va, vb, vc, wi * idx_scale, pad],
                         axis=2)
    wt = jnp.swapaxes(wt, 1, 2).astype(BF16)
    wn = jnp.concatenate([ka, ki, kb, kc], axis=2).astype(BF16)
    return wt, wn


@jax.jit
def kernel(x, c, ln1, ln2, w_mod, b_mod, w_in, qn_a, kn_a, qn_c, kn_c, sinks, g_out, w_o,
           w_gate, w_up, w_down):
    depth = w_in.shape[0]
    b, l, d = x.shape
    assert l % TOK_TILE == 0 and l >= ATT_TILE + WINDOW
    topk = min(TOPK_MAX, l // 4)
    att_scale = float(HEAD_DIM ** -0.5)

    tabs = _rope_tables(l)
    mod = _modulation(c, w_mod, b_mod)
    wt, wn = _split_w_in(w_in)
    w_o_b = w_o.astype(BF16)
    w_g_b, w_u_b, w_d_b = w_gate.astype(BF16), w_up.astype(BF16), w_down.astype(BF16)
    ones = jnp.ones((HEAD_DIM,), F32)

    for li in range(depth):
        gains = ((qn_a[li] * (att_scale * LOG2E)).reshape(HEAD_DIM, 1),
                 (qn_c[li] * att_scale).reshape(HEAD_DIM, 1),
                 jnp.concatenate([kn_a[li], ones]).reshape(1, LANES),
                 jnp.concatenate([kn_c[li], kn_c[li]]).reshape(1, LANES))
        (qa, qi, qb, qc, w, vat, vbt, vct, kaki, kb, kc) = _inproj(
            x, mod[li], ln1[li].reshape(1, d), wt[li], wn[li], tabs, gains)
        g = g_out[li].reshape(1, -1)
        oa = _dsa(qi, qa, w, kaki, vat, g[:, 0:256], topk)
        ob = _sb(qb, kb, vbt, g[:, 256:512])
        oc = _swa(sinks[li], qc, kc, vct, g[:, 512:1024])
        x = _merge_ffn(x, oa, ob, oc, mod[li], ln2[li].reshape(1, d),
                       w_o_b[li, 0:256], w_o_b[li, 256:512], w_o_b[li, 512:1024],
                       w_g_b[li], w_u_b[li], w_d_b[li])
    return x
```

```python
import functools

import jax
import jax.numpy as jnp
from jax import lax
from jax.experimental import pallas as pl
from jax.experimental.pallas import tpu as pltpu

F32 = jnp.float32
BF16 = jnp.bfloat16
I32 = jnp.int32

HEAD_DIM = 64
N_HEADS_A = 4
N_HEADS_B = 4
N_HEADS_C = 8
N_KV_C = 2
N_IDX_HEADS = 4
TOPK_MAX = 256
WINDOW = 128
ROPE_THETA = 10000.0
NORM_EPS = 1e-6
LANES = 128
ATT_TILE = 256
TOK_TILE = 512
VMEM_LIMIT = 48 * 1024 * 1024

NEG_BIG = -1e30
INT_MIN = -2 ** 31
INT_MAX = 2 ** 31 - 1
KEY_NEG_INF = -0x7F800000
IDX_BIG = 2 ** 30
LOG2E = 1.4426950408889634
PROBES_PER_CHECK = 3
FLOAT_PROBES = 24

R_QA, R_QI, R_QB, R_QC = 0, 256, 512, 768
R_VA, R_VB, R_VC, R_W = 1280, 1344, 1600, 1728
R_TOTAL = 1744
C_KAKI, C_KB, C_KC = 0, 128, 384
C_TOTAL = 512


def _cparams(sem):
    return pltpu.CompilerParams(dimension_semantics=sem, vmem_limit_bytes=VMEM_LIMIT)


def _mod_kernel(c_ref, w_ref, b_ref, o_ref):
    c = c_ref[...]
    ca = (c * jax.nn.sigmoid(c)).astype(BF16)
    y = jnp.dot(ca, w_ref[0].astype(BF16), preferred_element_type=F32)
    o_ref[0, 0] = y + b_ref[0, 0]


def _modulation(c, w_mod, b_mod):
    depth, d, _ = w_mod.shape
    b = c.shape[0]
    out = pl.pallas_call(
        _mod_kernel,
        grid=(depth, 6),
        in_specs=[
            pl.BlockSpec((b, d), lambda l, k: (0, 0)),
            pl.BlockSpec((1, d, d), lambda l, k: (l, 0, k)),
            pl.BlockSpec((1, 1, 1, d), lambda l, k: (l, k, 0, 0)),
        ],
        out_specs=pl.BlockSpec((1, 1, b, d), lambda l, k: (l, k, 0, 0)),
        out_shape=jax.ShapeDtypeStruct((depth, 6, b, d), F32),
        compiler_params=_cparams(("arbitrary", "arbitrary")),
        name="modulation",
    )(c, w_mod, b_mod.reshape(depth, 6, 1, d))
    return jnp.transpose(out, (0, 2, 1, 3))


def _inproj_kernel(x_ref, mod_ref, ln_ref, wt_ref, wn_ref, cost_ref, sint_ref, cosn_ref, sinn_ref,
                   gqa_ref, gqc_ref, gka_ref, gkc_ref,
                   qa_ref, qi_ref, qb_ref, qc_ref, w_ref, vat_ref, vbt_ref, vct_ref,
                   kaki_ref, kb_ref, kc_ref, *, tile):
    T = tile
    nc = x_ref.shape[1] // T
    gqa = gqa_ref[...]
    gqc = gqc_ref[...]
    lane = lax.broadcasted_iota(I32, (T, LANES), 1)
    left = lane < HEAD_DIM
    first = (lane & 32) == 0
    zero_half = jnp.zeros((HEAD_DIM, T), BF16)
    per_kv = N_HEADS_C // N_KV_C

    for c in range(nc):
        tok = slice(c * T, (c + 1) * T)
        x = x_ref[0, tok, :]
        ms = jnp.mean(x * x, axis=-1, keepdims=True)
        h = x * lax.rsqrt(ms + NORM_EPS) * ln_ref[...]
        h = h * (1.0 + mod_ref[0, 1:2, :]) + mod_ref[0, 0:1, :]
        hb = h.astype(BF16)
        pt = lax.dot_general(wt_ref[...], hb, (((1,), (1,)), ((), ())),
                             preferred_element_type=F32)
        pn = jnp.dot(hb, wn_ref[...], preferred_element_type=F32)

        cos_t = cost_ref[:, tok]
        sin_t = sint_ref[:, tok]

        def rope_t(y):
            y1, y2 = y[0:32], y[32:64]
            return jnp.concatenate([y1 * cos_t - y2 * sin_t, y2 * cos_t + y1 * sin_t], axis=0)

        def rms_t(y, g):
            m = jnp.mean(y * y, axis=0, keepdims=True)
            return y * lax.rsqrt(m + NORM_EPS) * g

        def put_padded(ref, lead, y, half, col):
            idx = (0, c) + lead
            ref[idx + (slice(64 * half, 64 * half + 64), slice(col * T, (col + 1) * T))] = \
                y.astype(BF16)
            ref[idx + (slice(64 * (1 - half), 64 * (1 - half) + 64),
                       slice(col * T, (col + 1) * T))] = zero_half

        for hh in range(N_HEADS_A):
            put_padded(qa_ref, (), rope_t(rms_t(pt[R_QA + 64 * hh:R_QA + 64 * hh + 64], gqa)), 0, hh)
        for hh in range(N_IDX_HEADS):
            put_padded(qi_ref, (), rope_t(pt[R_QI + 64 * hh:R_QI + 64 * hh + 64]), 1, hh)
        for hh in range(N_HEADS_B):
            put_padded(qb_ref, (), pt[R_QB + 64 * hh:R_QB + 64 * hh + 64], hh % 2, hh)
        for hh in range(N_HEADS_C):
            g = hh // per_kv
            put_padded(qc_ref, (g,), rope_t(rms_t(pt[R_QC + 64 * hh:R_QC + 64 * hh + 64], gqc)),
                       g, hh % per_kv)
        vat_ref[0, c] = pt[R_VA:R_VA + 64].astype(BF16)
        vbt_ref[0, c] = pt[R_VB:R_VB + 256].astype(BF16)
        for hh in range(N_IDX_HEADS):
            w_ref[0, c, :, hh * T:(hh + 1) * T] = pt[R_W + hh:R_W + hh + 1]
        for cc in range(T // LANES):
            vct_ref[0, c * (T // LANES) + cc] = \
                pt[R_VC:R_VC + 128, cc * LANES:(cc + 1) * LANES].astype(BF16)

        cos_n = cosn_ref[tok, :]
        sin_n = sinn_ref[tok, :]

        def rope_n(y):
            rot = jnp.where(first, pltpu.roll(y, 96, 1), pltpu.roll(y, 32, 1))
            return y * cos_n + rot * sin_n

        def head_rsqrt(y):
            sq = y * y
            s0 = jnp.sum(jnp.where(left, sq, 0.0), axis=-1, keepdims=True)
            s1 = jnp.sum(jnp.where(left, 0.0, sq), axis=-1, keepdims=True)
            return lax.rsqrt(jnp.where(left, s0, s1) * (1.0 / HEAD_DIM) + NORM_EPS)

        y = pn[:, C_KAKI:C_KAKI + 128]
        fac = jnp.where(left, head_rsqrt(y), 1.0) * gka_ref[...]
        kaki_ref[0, tok, :] = rope_n(y * fac).astype(BF16)
        kb_ref[0, tok, :] = pn[:, C_KB:C_KB + 256].astype(BF16)
        y = pn[:, C_KC:C_KC + 128]
        kc_ref[0, tok, :] = rope_n(y * head_rsqrt(y) * gkc_ref[...]).astype(BF16)


def _inproj(x, mod_l, ln, wt, wn, tabs, gains):
    b, l, d = x.shape
    T = ATT_TILE
    tm = TOK_TILE
    nq = l // T
    nc = tm // T
    cos_t, sin_t, cos_n, sin_n = tabs
    gqa, gqc, gka, gkc = gains
    const2 = lambda bb, i: (0, 0)
    out_shape = (
        jax.ShapeDtypeStruct((b, nq, 128, 4 * T), BF16),
        jax.ShapeDtypeStruct((b, nq, 128, 4 * T), BF16),
        jax.ShapeDtypeStruct((b, nq, 128, 4 * T), BF16),
        jax.ShapeDtypeStruct((b, nq, 2, 128, 4 * T), BF16),
        jax.ShapeDtypeStruct((b, nq, 1, 4 * T), F32),
        jax.ShapeDtypeStruct((b, nq, 64, T), BF16),
        jax.ShapeDtypeStruct((b, nq, 256, T), BF16),
        jax.ShapeDtypeStruct((b, l // LANES, 128, LANES), BF16),
        jax.ShapeDtypeStruct((b, l, 128), BF16),
        jax.ShapeDtypeStruct((b, l, 256), BF16),
        jax.ShapeDtypeStruct((b, l, 128), BF16),
    )
    out_specs = (
        pl.BlockSpec((1, nc, 128, 4 * T), lambda bb, i: (bb, i, 0, 0)),
        pl.BlockSpec((1, nc, 128, 4 * T), lambda bb, i: (bb, i, 0, 0)),
        pl.BlockSpec((1, nc, 128, 4 * T), lambda bb, i: (bb, i, 0, 0)),
        pl.BlockSpec((1, nc, 2, 128, 4 * T), lambda bb, i: (bb, i, 0, 0, 0)),
        pl.BlockSpec((1, nc, 1, 4 * T), lambda bb, i: (bb, i, 0, 0)),
        pl.BlockSpec((1, nc, 64, T), lambda bb, i: (bb, i, 0, 0)),
        pl.BlockSpec((1, nc, 256, T), lambda bb, i: (bb, i, 0, 0)),
        pl.BlockSpec((1, tm // LANES, 128, LANES), lambda bb, i: (bb, i, 0, 0)),
        pl.BlockSpec((1, tm, 128), lambda bb, i: (bb, i, 0)),
        pl.BlockSpec((1, tm, 256), lambda bb, i: (bb, i, 0)),
        pl.BlockSpec((1, tm, 128), lambda bb, i: (bb, i, 0)),
    )
    in_specs = [
        pl.BlockSpec((1, tm, d), lambda bb, i: (bb, i, 0)),
        pl.BlockSpec((1, 6, d), lambda bb, i: (bb, 0, 0)),
        pl.BlockSpec((1, d), const2),
        pl.BlockSpec((R_TOTAL, d), const2),
        pl.BlockSpec((d, C_TOTAL), const2),
        pl.BlockSpec((32, tm), lambda bb, i: (0, i)),
        pl.BlockSpec((32, tm), lambda bb, i: (0, i)),
        pl.BlockSpec((tm, LANES), lambda bb, i: (i, 0)),
        pl.BlockSpec((tm, LANES), lambda bb, i: (i, 0)),
        pl.BlockSpec((HEAD_DIM, 1), const2),
        pl.BlockSpec((HEAD_DIM, 1), const2),
        pl.BlockSpec((1, LANES), const2),
        pl.BlockSpec((1, LANES), const2),
    ]
    return pl.pallas_call(
        functools.partial(_inproj_kernel, tile=T),
        grid=(b, l // tm),
        in_specs=in_specs,
        out_specs=out_specs,
        out_shape=out_shape,
        compiler_params=_cparams(("parallel", "arbitrary")),
        name="inproj",
    )(x, mod_l, ln, wt, wn, cos_t, sin_t, cos_n, sin_n, gqa, gqc, gka, gkc)


def _store_heads(o_ref, g_ref, heads_t):
    for p in range(len(heads_t) // 2):
        parts = []
        for o in heads_t[2 * p:2 * p + 2]:
            ms = jnp.mean(o * o, axis=0, keepdims=True)
            parts.append(o * lax.rsqrt(ms + NORM_EPS))
        ot = jnp.concatenate(parts, axis=0).T
        o_ref[0, :, 128 * p:128 * p + 128] = (ot * g_ref[:, 128 * p:128 * p + 128]).astype(BF16)


def _float_key(x):
    kb = pltpu.bitcast(x, I32)
    return jnp.where(kb < 0, INT_MIN - kb, kb)


def _key_float(k):
    return pltpu.bitcast(jnp.where(k < 0, INT_MIN - k, k), F32)


def _fold8(x, op):
    t = x.shape[0]
    return op(x.reshape(t // 8, 8, x.shape[1]), axis=0)


def _dsa_kernel(qi_ref, qa_ref, w_ref, kaki_ref, vat_ref, g_ref, o_ref,
                key_scr, s_scr, r_scr, ib_scr, m_scr, l_scr, acc_scr, *, tile, topk):
    T = tile
    K = topk
    i = pl.program_id(1)
    nk = i + 1
    rowk = lax.broadcasted_iota(I32, (T, T), 0)
    colq = lax.broadcasted_iota(I32, (T, T), 1)

    def key_chunk(j):
        return kaki_ref[0, pl.ds(pl.multiple_of(j * T, T), T), :]

    def index_dots(j):
        kc = key_chunk(j)
        for h in range(N_IDX_HEADS):
            s_scr[h] = jnp.dot(kc, qi_ref[0, 0, :, h * T:(h + 1) * T], preferred_element_type=F32)

    def attn_dots(j):
        kc = key_chunk(j)
        for h in range(N_HEADS_A):
            r_scr[h] = jnp.dot(kc, qa_ref[0, 0, :, h * T:(h + 1) * T], preferred_element_type=F32)

    def score_chunk(j, nxt, diag, mx8, mn8):
        sc = None
        for h in range(N_IDX_HEADS):
            r = jnp.maximum(s_scr[h], 0.0) * w_ref[0, 0, :, h * T:(h + 1) * T]
            sc = r if sc is None else sc + r
        index_dots(nxt)
        key = _float_key(sc)
        if diag:
            causal = rowk <= colq
            key_hi = jnp.where(causal, key, KEY_NEG_INF)
            key_lo = jnp.where(causal, key, INT_MAX)
        else:
            key_hi = key_lo = key
        key_scr[j] = key_hi
        return (jnp.maximum(mx8, _fold8(key_hi, jnp.max)),
                jnp.minimum(mn8, _fold8(key_lo, jnp.min)))

    index_dots(i)
    attn_dots(0)
    mx8, mn8 = score_chunk(i, 0, True, jnp.full((8, T), INT_MIN, I32),
                           jnp.full((8, T), INT_MAX, I32))
    mx8, mn8 = lax.fori_loop(
        0, i, lambda j, c: score_chunk(j, jnp.minimum(j + 1, i - 1), False, *c), (mx8, mn8))
    rmax = jnp.max(mx8, axis=0, keepdims=True)
    rmin = jnp.min(mn8, axis=0, keepdims=True)

    def count_ge(mid):
        def body(j, acc):
            return acc + _fold8(jnp.where(key_scr[j] >= mid, 1, 0), jnp.sum)
        acc = lax.fori_loop(0, nk, body, jnp.zeros((8, T), I32))
        return jnp.sum(acc, axis=0, keepdims=True)

    tq = i * T + lax.broadcasted_iota(I32, (1, T), 1)
    big = tq >= K
    lo0 = jnp.where(big, rmin, KEY_NEG_INF)
    hi0 = jnp.where(big, rmax + 1, KEY_NEG_INF + 1)
    clo0 = tq + 1
    chi0 = jnp.zeros((1, T), I32)

    def probe(it, lo, hi, clo, chi):
        fmid = _float_key(0.5 * _key_float(lo) + 0.5 * _key_float(hi))
        kmid = (lo >> 1) + (hi >> 1) + (lo & hi & 1)
        inside = jnp.where(fmid > lo, jnp.where(fmid < hi, it, FLOAT_PROBES), FLOAT_PROBES)
        mid = jnp.where(inside < FLOAT_PROBES, fmid, kmid)
        mid = jnp.where(lo == 0, jnp.where(hi > 1, 1, mid), mid)
        mid = jnp.where(lo < 0, jnp.where(hi > 0, 0, mid), mid)
        c = count_ge(mid)
        ge = c >= K
        return (jnp.where(ge, mid, lo), jnp.where(ge, hi, mid),
                jnp.where(ge, c, clo), jnp.where(ge, chi, c))

    def bis_body(st):
        _, it, lo, hi, clo, chi = st
        for _ in range(PROBES_PER_CHECK):
            lo, hi, clo, chi = probe(it, lo, hi, clo, chi)
            it = it + 1
        open_ = jnp.where(clo == K, 0.0, jnp.where((hi - 1) == lo, 0.0, 1.0))
        return jnp.sum(open_), it, lo, hi, clo, chi

    _, _, lo, hi, clo, chi = lax.while_loop(
        lambda st: st[0] > 0.0, bis_body,
        (jnp.sum(jnp.where(big, 1.0, 0.0)), jnp.zeros((1, T), I32), lo0, hi0, clo0, chi0))

    tie = jnp.where(big, jnp.where(clo > K, 1.0, 0.0), 0.0)
    thr = jnp.maximum(lo, KEY_NEG_INF + 1)
    need = (K - chi).astype(F32)
    ib_scr[...] = jnp.full((1, T), IDX_BIG, I32)

    @pl.when(jnp.sum(tie) > 0.0)
    def _():
        before = jnp.where(colq < rowk, 1.0, 0.0).astype(BF16)

        def body(j, carry):
            run, ibm8 = carry
            eq = jnp.where(key_scr[j] == thr, 1.0, 0.0)
            pc = jnp.dot(before, eq.astype(BF16), preferred_element_type=F32) + run
            idx1 = (rowk + (j * T + 1)).astype(F32)
            taken = jnp.where(pc < need, eq * idx1, 0.0)
            return (run + jnp.sum(_fold8(eq, jnp.sum), axis=0, keepdims=True),
                    jnp.maximum(ibm8, _fold8(taken, jnp.max)))

        _, ibm8 = lax.fori_loop(0, nk, body, (jnp.zeros((1, T), F32), jnp.zeros((8, T), F32)))
        ibm = jnp.max(ibm8, axis=0, keepdims=True).astype(I32)
        ib_scr[...] = jnp.where(tie > 0.0, ibm, IDX_BIG)

    ib = ib_scr[...]

    m_scr[...] = jnp.full(m_scr.shape, NEG_BIG, F32)
    l_scr[...] = jnp.zeros(l_scr.shape, F32)
    acc_scr[...] = jnp.zeros(acc_scr.shape, F32)

    def attn_body(j, carry):
        t_el = jnp.where((rowk + j * T) < ib, thr, thr + 1)
        sel = key_scr[j] >= t_el
        vt = vat_ref[0, j]
        m_new = []
        for h in range(N_HEADS_A):
            s = jnp.where(sel, r_scr[h], NEG_BIG)
            s_scr[h] = s
            mc = jnp.max(_fold8(s, jnp.max), axis=0, keepdims=True)
            m_new.append(jnp.maximum(m_scr[h], mc))
        attn_dots(jnp.minimum(j + 1, i))
        for h in range(N_HEADS_A):
            p = jnp.exp2(s_scr[h] - m_new[h])
            alpha = jnp.exp2(m_scr[h] - m_new[h])
            l_scr[h] = alpha * l_scr[h] + jnp.sum(_fold8(p, jnp.sum), axis=0, keepdims=True)
            acc_scr[h] = alpha * acc_scr[h] + jnp.dot(vt, p.astype(BF16),
                                                      preferred_element_type=F32)
            m_scr[h] = m_new[h]
        return carry

    lax.fori_loop(0, nk, attn_body, 0)
    _store_heads(o_ref, g_ref, [acc_scr[h] / l_scr[h] for h in range(N_HEADS_A)])


def _dsa(qi, qa, w, kaki, vat, g, topk):
    b, nq, _, _ = qa.shape
    T = ATT_TILE
    l = nq * T
    return pl.pallas_call(
        functools.partial(_dsa_kernel, tile=T, topk=topk),
        grid=(b, nq),
        in_specs=[
            pl.BlockSpec((1, 1, 128, 4 * T), lambda bb, i: (bb, i, 0, 0)),
            pl.BlockSpec((1, 1, 128, 4 * T), lambda bb, i: (bb, i, 0, 0)),
            pl.BlockSpec((1, 1, 1, 4 * T), lambda bb, i: (bb, i, 0, 0)),
            pl.BlockSpec((1, l, 128), lambda bb, i: (bb, 0, 0)),
            pl.BlockSpec((1, nq, 64, T), lambda bb, i: (bb, 0, 0, 0)),
            pl.BlockSpec((1, 256), lambda bb, i: (0, 0)),
        ],
        out_specs=pl.BlockSpec((1, T, 256), lambda bb, i: (bb, i, 0)),
        out_shape=jax.ShapeDtypeStruct((b, l, 256), BF16),
        scratch_shapes=[
            pltpu.VMEM((nq, T, T), I32),
            pltpu.VMEM((N_HEADS_A, T, T), F32),
            pltpu.VMEM((N_HEADS_A, T, T), F32),
            pltpu.VMEM((1, T), I32),
            pltpu.VMEM((N_HEADS_A, 1, T), F32),
            pltpu.VMEM((N_HEADS_A, 1, T), F32),
            pltpu.VMEM((N_HEADS_A, HEAD_DIM, T), F32),
        ],
        compiler_params=_cparams(("parallel", "arbitrary")),
        name="dsa",
    )(qi, qa, w, kaki, vat, g)


def _sb_kernel(qb_ref, kb_ref, vbt_ref, g_ref, o_ref, run_scr, acc_scr, z_scr, e_scr, c_scr, hl_scr,
               *, tile):
    T = tile
    i = pl.program_id(1)
    rowk = lax.broadcasted_iota(I32, (T, T), 0)
    colq = lax.broadcasted_iota(I32, (T, T), 1)
    strict = rowk < colq
    col2 = lax.broadcasted_iota(I32, (T, 2 * T), 1)
    row2 = lax.broadcasted_iota(I32, (T, 2 * T), 0)
    tmat2 = jnp.where((col2 & (T - 1)) > row2, 1.0, 0.0).astype(BF16)
    run_scr[...] = jnp.zeros(run_scr.shape, F32)
    acc_scr[...] = jnp.zeros(acc_scr.shape, F32)

    def scores(j):
        row0 = pl.multiple_of(j * T, T)
        for h in range(N_HEADS_B):
            p = h // 2
            kc = kb_ref[0, pl.ds(row0, T), 128 * p:128 * p + 128]
            z_scr[h] = jnp.dot(kc, qb_ref[0, 0, :, h * T:(h + 1) * T],
                               preferred_element_type=F32)

    def chunk(j, diag):
        tots = []
        for h in range(N_HEADS_B):
            z = z_scr[h]
            nz = -z
            lk = jnp.minimum(nz, 0.0) - jnp.log2(1.0 + jnp.exp2(jnp.minimum(z, nz)))
            if diag:
                lk = jnp.where(strict, lk, 0.0)
            hi = lk.astype(BF16)
            hl_scr[h, 0:T, :] = hi
            hl_scr[h, T:2 * T, :] = (lk - hi.astype(F32)).astype(BF16)
            e_scr[h] = z + lk
            tots.append(jnp.sum(_fold8(lk, jnp.sum), axis=0, keepdims=True))
            c_scr[h] = jnp.dot(tmat2, hl_scr[h], preferred_element_type=F32)
        scores(jnp.maximum(j - 1, 0))
        for h in range(N_HEADS_B):
            a = jnp.exp2(e_scr[h] + c_scr[h] + run_scr[h])
            if diag:
                a = jnp.where(strict, a, 0.0)
            acc_scr[h] = acc_scr[h] + jnp.dot(vbt_ref[0, j, 64 * h:64 * h + 64, :], a.astype(BF16),
                                              preferred_element_type=F32)
            run_scr[h] = run_scr[h] + tots[h]

    scores(i)
    chunk(i, True)

    def body(jj, carry):
        chunk(i - 1 - jj, False)
        return carry

    lax.fori_loop(0, i, body, 0)
    _store_heads(o_ref, g_ref, [acc_scr[h] for h in range(N_HEADS_B)])


def _sb(qb, kb, vbt, g):
    b, nq, _, _ = qb.shape
    T = ATT_TILE
    l = nq * T
    return pl.pallas_call(
        functools.partial(_sb_kernel, tile=T),
        grid=(b, nq),
        in_specs=[
            pl.BlockSpec((1, 1, 128, 4 * T), lambda bb, i: (bb, i, 0, 0)),
            pl.BlockSpec((1, l, 256), lambda bb, i: (bb, 0, 0)),
            pl.BlockSpec((1, nq, 256, T), lambda bb, i: (bb, 0, 0, 0)),
            pl.BlockSpec((1, 256), lambda bb, i: (0, 0)),
        ],
        out_specs=pl.BlockSpec((1, T, 256), lambda bb, i: (bb, i, 0)),
        out_shape=jax.ShapeDtypeStruct((b, l, 256), BF16),
        scratch_shapes=[
            pltpu.VMEM((N_HEADS_B, 1, T), F32),
            pltpu.VMEM((N_HEADS_B, HEAD_DIM, T), F32),
            pltpu.VMEM((N_HEADS_B, T, T), F32),
            pltpu.VMEM((N_HEADS_B, T, T), F32),
            pltpu.VMEM((N_HEADS_B, T, T), F32),
            pltpu.VMEM((N_HEADS_B, 2 * T, T), BF16),
        ],
        compiler_params=_cparams(("parallel", "arbitrary")),
        name="stickbreak",
    )(qb, kb, vbt, g)


def _swa_kernel(sink_ref, qc_ref, kc_ref, vct_ref, g_ref, o_ref, s_scr, *, tile):
    T = tile
    NKEY = T + WINDOW
    i = pl.program_id(1)
    start = pl.multiple_of(jnp.maximum(i * T - WINDOW, 0), WINDOW)
    kidx = start + lax.broadcasted_iota(I32, (NKEY, T), 0)
    tq = i * T + lax.broadcasted_iota(I32, (NKEY, T), 1)
    d = tq - kidx
    band = (d >= 0) & (d < WINDOW)
    kc = kc_ref[0, pl.ds(start, NKEY), :]
    c0 = start // WINDOW
    per_kv = N_HEADS_C // N_KV_C
    for h in range(N_HEADS_C):
        s_scr[h] = jnp.dot(kc, qc_ref[0, 0, h // per_kv, :, (h % per_kv) * T:(h % per_kv + 1) * T],
                           preferred_element_type=F32)
    outs = []
    for h in range(N_HEADS_C):
        g = h // per_kv
        s = jnp.where(band, s_scr[h], NEG_BIG)
        sink = sink_ref[h] * LOG2E
        m = jnp.maximum(jnp.max(_fold8(s, jnp.max), axis=0, keepdims=True), sink)
        e = jnp.exp2(s - m)
        den = jnp.sum(_fold8(e, jnp.sum), axis=0, keepdims=True) + jnp.exp2(sink - m)
        p = e.astype(BF16)
        o = None
        for c in range(NKEY // WINDOW):
            part = jnp.dot(vct_ref[0, c0 + c, 64 * g:64 * g + 64, :],
                           p[c * WINDOW:(c + 1) * WINDOW, :], preferred_element_type=F32)
            o = part if o is None else o + part
        outs.append(o / den)
    _store_heads(o_ref, g_ref, outs)


def _swa(sinks, qc, kc, vct, g):
    b, nq = qc.shape[:2]
    T = ATT_TILE
    l = nq * T
    return pl.pallas_call(
        functools.partial(_swa_kernel, tile=T),
        grid=(b, nq),
        in_specs=[
            pl.BlockSpec(memory_space=pltpu.SMEM),
            pl.BlockSpec((1, 1, 2, 128, 4 * T), lambda bb, i: (bb, i, 0, 0, 0)),
            pl.BlockSpec((1, l, 128), lambda bb, i: (bb, 0, 0)),
            pl.BlockSpec((1, l // LANES, 128, LANES), lambda bb, i: (bb, 0, 0, 0)),
            pl.BlockSpec((1, 512), lambda bb, i: (0, 0)),
        ],
        out_specs=pl.BlockSpec((1, T, 512), lambda bb, i: (bb, i, 0)),
        out_shape=jax.ShapeDtypeStruct((b, l, 512), BF16),
        scratch_shapes=[pltpu.VMEM((N_HEADS_C, T + WINDOW, T), F32)],
        compiler_params=_cparams(("parallel", "arbitrary")),
        name="swa",
    )(sinks, qc, kc, vct, g)


def _ffn_chunks(d_ff):
    chunks, c0 = [], 0
    while c0 < d_ff:
        cw = min(1024, d_ff - c0)
        chunks.append((c0, cw))
        c0 += cw
    return chunks


def _merge_ffn_kernel(x_ref, oa_ref, ob_ref, oc_ref, mod_ref, ln_ref, woa_ref, wob_ref, woc_ref,
                      wg_ref, wu_ref, wd_ref, out_ref):
    x = x_ref[0]
    y = (jnp.dot(oa_ref[0], woa_ref[...], preferred_element_type=F32)
         + jnp.dot(ob_ref[0], wob_ref[...], preferred_element_type=F32)
         + jnp.dot(oc_ref[0], woc_ref[...], preferred_element_type=F32))
    x1 = x + mod_ref[0, 2:3, :] * y
    ms = jnp.mean(x1 * x1, axis=-1, keepdims=True)
    h = x1 * lax.rsqrt(ms + NORM_EPS) * ln_ref[...]
    hb = (h * (1.0 + mod_ref[0, 4:5, :]) + mod_ref[0, 3:4, :]).astype(BF16)
    acc = None
    for c0, cw in _ffn_chunks(wg_ref.shape[1]):
        gt = jnp.dot(hb, wg_ref[:, c0:c0 + cw], preferred_element_type=F32)
        up = jnp.dot(hb, wu_ref[:, c0:c0 + cw], preferred_element_type=F32)
        act = (gt * jax.nn.sigmoid(gt) * up).astype(BF16)
        part = jnp.dot(act, wd_ref[c0:c0 + cw, :], preferred_element_type=F32)
        acc = part if acc is None else acc + part
    out_ref[0] = x1 + mod_ref[0, 5:6, :] * acc


def _merge_ffn(x, oa, ob, oc, mod_l, ln, woa, wob, woc, wg, wu, wd):
    b, l, d = x.shape
    tm = TOK_TILE
    dff = wg.shape[1]
    const2 = lambda bb, i: (0, 0)
    once = pl.Buffered(1)
    return pl.pallas_call(
        _merge_ffn_kernel,
        grid=(b, l // tm),
        in_specs=[
            pl.BlockSpec((1, tm, d), lambda bb, i: (bb, i, 0)),
            pl.BlockSpec((1, tm, 256), lambda bb, i: (bb, i, 0)),
            pl.BlockSpec((1, tm, 256), lambda bb, i: (bb, i, 0)),
            pl.BlockSpec((1, tm, 512), lambda bb, i: (bb, i, 0)),
            pl.BlockSpec((1, 6, d), lambda bb, i: (bb, 0, 0)),
            pl.BlockSpec((1, d), const2),
            pl.BlockSpec((256, d), const2, pipeline_mode=once),
            pl.BlockSpec((256, d), const2, pipeline_mode=once),
            pl.BlockSpec((512, d), const2, pipeline_mode=once),
            pl.BlockSpec((d, dff), const2, pipeline_mode=once),
            pl.BlockSpec((d, dff), const2, pipeline_mode=once),
            pl.BlockSpec((dff, d), const2, pipeline_mode=once),
        ],
        out_specs=pl.BlockSpec((1, tm, d), lambda bb, i: (bb, i, 0)),
        out_shape=jax.ShapeDtypeStruct((b, l, d), F32),
        compiler_params=_cparams(("parallel", "arbitrary")),
        name="merge_ffn",
    )(x, oa, ob, oc, mod_l, ln, woa, wob, woc, wg, wu, wd)


def _rope_tables(l):
    inv = 1.0 / (ROPE_THETA ** (jnp.arange(0, HEAD_DIM, 2, dtype=F32) / HEAD_DIM))
    ang = jnp.arange(l, dtype=F32)[:, None] * inv[None, :]
    cos, sin = jnp.cos(ang), jnp.sin(ang)
    cos_n = jnp.tile(cos, (1, 4))
    sin_n = jnp.concatenate([-sin, sin, -sin, sin], axis=1)
    return cos.T, sin.T, cos_n, sin_n


def _split_w_in(w_in):
    widths = (256, 64, 64, 256, 64, 4, 256, 256, 256, 512, 128, 128)
    offs = [0]
    for w in widths:
        offs.append(offs[-1] + w)
    qa, ka, va, qi, ki, wi, qb, kb, vb, qc, kc, vc = [w_in[:, :, offs[k]:offs[k + 1]]
                                                      for k in range(len(widths))]
    idx_scale = float((N_IDX_HEADS * HEAD_DIM) ** -0.5)
    att_scale = float(HEAD_DIM ** -0.5)
    pad = jnp.zeros(wi.shape[:2] + (R_TOTAL - R_W - N_IDX_HEADS,), w_in.dtype)
    wt = jnp.concatenate([qa, qi, qb * (att_scale * LOG2E), qc, va, vb, vc, wi * idx_scale, pad],
                         axis=2)
    wt = jnp.swapaxes(wt, 1, 2).astype(BF16)
    wn = jnp.concatenate([ka, ki, kb, kc], axis=2).astype(BF16)
    return wt, wn


@jax.jit
def kernel(x, c, ln1, ln2, w_mod, b_mod, w_in, qn_a, kn_a, qn_c, kn_c, sinks, g_out, w_o,
           w_gate, w_up, w_down):
    depth = w_in.shape[0]
    b, l, d = x.shape
    assert l % TOK_TILE == 0 and l >= ATT_TILE + WINDOW
    topk = min(TOPK_MAX, l // 4)
    att_scale = float(HEAD_DIM ** -0.5)

    tabs = _rope_tables(l)
    mod = _modulation(c, w_mod, b_mod)
    wt, wn = _split_w_in(w_in)
    w_o_b = w_o.astype(BF16)
    w_g_b, w_u_b, w_d_b = w_gate.astype(BF16), w_up.astype(BF16), w_down.astype(BF16)
    ones = jnp.ones((HEAD_DIM,), F32)

    for li in range(depth):
        gains = ((qn_a[li] * (att_scale * LOG2E)).reshape(HEAD_DIM, 1),
                 (qn_c[li] * (att_scale * LOG2E)).reshape(HEAD_DIM, 1),
                 jnp.concatenate([kn_a[li], ones]).reshape(1, LANES),
                 jnp.concatenate([kn_c[li], kn_c[li]]).reshape(1, LANES))
        (qa, qi, qb, qc, w, vat, vbt, vct, kaki, kb, kc) = _inproj(
            x, mod[li], ln1[li].reshape(1, d), wt[li], wn[li], tabs, gains)
        g = g_out[li].reshape(1, -1)
        oa = _dsa(qi, qa, w, kaki, vat, g[:, 0:256], topk)
        ob = _sb(qb, kb, vbt, g[:, 256:512])
        oc = _swa(sinks[li], qc, kc, vct, g[:, 512:1024])
        x = _merge_ffn(x, oa, ob, oc, mod[li], ln2[li].reshape(1, d),
                       w_o_b[li, 0:256], w_o_b[li, 256:512], w_o_b[li, 512:1024],
                       w_g_b[li], w_u_b[li], w_d_b[li])
    return x
```

```python
import functools

import jax
import jax.numpy as jnp
from jax import lax
from jax.experimental import pallas as pl
from jax.experimental.pallas import tpu as pltpu

F32 = jnp.float32
BF16 = jnp.bfloat16
I32 = jnp.int32

HEAD_DIM = 64
N_HEADS_A = 4
N_HEADS_B = 4
N_HEADS_C = 8
N_KV_C = 2
N_IDX_HEADS = 4
TOPK_MAX = 256
WINDOW = 128
ROPE_THETA = 10000.0
NORM_EPS = 1e-6
LANES = 128
ATT_TILE = 256
TOK_TILE = 512
VMEM_LIMIT = 48 * 1024 * 1024

NEG_BIG = -1e30
INT_MIN = -2 ** 31
INT_MAX = 2 ** 31 - 1
KEY_NEG_INF = -0x7F800000
IDX_BIG = 2 ** 30
LOG2E = 1.4426950408889634
FIRST_PROBES = 12
PROBES_PER_CHECK = 2
FLOAT_PROBES = 30

R_QA, R_QI, R_QB, R_QC = 0, 256, 512, 768
R_VA, R_VB, R_VC, R_W = 1280, 1344, 1600, 1728
R_TOTAL = 1744
C_KAKI, C_KB, C_KC = 0, 128, 384
C_TOTAL = 512


def _cparams(sem):
    return pltpu.CompilerParams(dimension_semantics=sem, vmem_limit_bytes=VMEM_LIMIT)


def _mod_kernel(c_ref, w_ref, b_ref, o_ref):
    c = c_ref[...]
    ca = (c * jax.nn.sigmoid(c)).astype(BF16)
    y = jnp.dot(ca, w_ref[0].astype(BF16), preferred_element_type=F32)
    o_ref[0, 0] = y + b_ref[0, 0]


def _modulation(c, w_mod, b_mod):
    depth, d, _ = w_mod.shape
    b = c.shape[0]
    out = pl.pallas_call(
        _mod_kernel,
        grid=(depth, 6),
        in_specs=[
            pl.BlockSpec((b, d), lambda l, k: (0, 0)),
            pl.BlockSpec((1, d, d), lambda l, k: (l, 0, k)),
            pl.BlockSpec((1, 1, 1, d), lambda l, k: (l, k, 0, 0)),
        ],
        out_specs=pl.BlockSpec((1, 1, b, d), lambda l, k: (l, k, 0, 0)),
        out_shape=jax.ShapeDtypeStruct((depth, 6, b, d), F32),
        compiler_params=_cparams(("arbitrary", "arbitrary")),
        name="modulation",
    )(c, w_mod, b_mod.reshape(depth, 6, 1, d))
    return jnp.transpose(out, (0, 2, 1, 3))


def _inproj_kernel(x_ref, mod_ref, ln_ref, wt_ref, wn_ref, cost_ref, sint_ref, cosn_ref, sinn_ref,
                   gqa_ref, gqc_ref, gka_ref, gkc_ref,
                   qa_ref, qi_ref, qb_ref, qc_ref, w_ref, vat_ref, vbt_ref, vct_ref,
                   kaki_ref, kb_ref, kc_ref, *, tile):
    T = tile
    nc = x_ref.shape[1] // T
    gqa = gqa_ref[...]
    gqc = gqc_ref[...]
    lane = lax.broadcasted_iota(I32, (T, LANES), 1)
    left = lane < HEAD_DIM
    first = (lane & 32) == 0
    zero_half = jnp.zeros((HEAD_DIM, T), BF16)
    per_kv = N_HEADS_C // N_KV_C

    for c in range(nc):
        tok = slice(c * T, (c + 1) * T)
        x = x_ref[0, tok, :]
        ms = jnp.mean(x * x, axis=-1, keepdims=True)
        h = x * lax.rsqrt(ms + NORM_EPS) * ln_ref[...]
        h = h * (1.0 + mod_ref[0, 1:2, :]) + mod_ref[0, 0:1, :]
        hb = h.astype(BF16)
        pt = lax.dot_general(wt_ref[...], hb, (((1,), (1,)), ((), ())),
                             preferred_element_type=F32)
        pn = jnp.dot(hb, wn_ref[...], preferred_element_type=F32)

        cos_t = cost_ref[:, tok]
        sin_t = sint_ref[:, tok]

        def rope_t(y):
            y1, y2 = y[0:32], y[32:64]
            return jnp.concatenate([y1 * cos_t - y2 * sin_t, y2 * cos_t + y1 * sin_t], axis=0)

        def rms_t(y, g):
            m = jnp.mean(y * y, axis=0, keepdims=True)
            return y * lax.rsqrt(m + NORM_EPS) * g

        def put_padded(ref, lead, y, half, col):
            idx = (0, c) + lead
            ref[idx + (slice(64 * half, 64 * half + 64), slice(col * T, (col + 1) * T))] = \
                y.astype(BF16)
            ref[idx + (slice(64 * (1 - half), 64 * (1 - half) + 64),
                       slice(col * T, (col + 1) * T))] = zero_half

        for hh in range(N_HEADS_A):
            put_padded(qa_ref, (), rope_t(rms_t(pt[R_QA + 64 * hh:R_QA + 64 * hh + 64], gqa)), 0, hh)
        for hh in range(N_IDX_HEADS):
            put_padded(qi_ref, (), rope_t(pt[R_QI + 64 * hh:R_QI + 64 * hh + 64]), 1, hh)
        for hh in range(N_HEADS_B):
            put_padded(qb_ref, (), pt[R_QB + 64 * hh:R_QB + 64 * hh + 64], hh % 2, hh)
        for hh in range(N_HEADS_C):
            g = hh // per_kv
            put_padded(qc_ref, (g,), rope_t(rms_t(pt[R_QC + 64 * hh:R_QC + 64 * hh + 64], gqc)),
                       g, hh % per_kv)
        vat_ref[0, c] = pt[R_VA:R_VA + 64].astype(BF16)
        vbt_ref[0, c] = pt[R_VB:R_VB + 256].astype(BF16)
        for hh in range(N_IDX_HEADS):
            w_ref[0, c, :, hh * T:(hh + 1) * T] = pt[R_W + hh:R_W + hh + 1]
        for cc in range(T // LANES):
            vct_ref[0, c * (T // LANES) + cc] = \
                pt[R_VC:R_VC + 128, cc * LANES:(cc + 1) * LANES].astype(BF16)

        cos_n = cosn_ref[tok, :]
        sin_n = sinn_ref[tok, :]

        def rope_n(y):
            rot = jnp.where(first, pltpu.roll(y, 96, 1), pltpu.roll(y, 32, 1))
            return y * cos_n + rot * sin_n

        def head_rsqrt(y):
            sq = y * y
            s0 = jnp.sum(jnp.where(left, sq, 0.0), axis=-1, keepdims=True)
            s1 = jnp.sum(jnp.where(left, 0.0, sq), axis=-1, keepdims=True)
            return lax.rsqrt(jnp.where(left, s0, s1) * (1.0 / HEAD_DIM) + NORM_EPS)

        y = pn[:, C_KAKI:C_KAKI + 128]
        fac = jnp.where(left, head_rsqrt(y), 1.0) * gka_ref[...]
        kaki_ref[0, tok, :] = rope_n(y * fac).astype(BF16)
        kb_ref[0, tok, :] = pn[:, C_KB:C_KB + 256].astype(BF16)
        y = pn[:, C_KC:C_KC + 128]
        kc_ref[0, tok, :] = rope_n(y * head_rsqrt(y) * gkc_ref[...]).astype(BF16)


def _inproj(x, mod_l, ln, wt, wn, tabs, gains):
    b, l, d = x.shape
    T = ATT_TILE
    tm = TOK_TILE
    nq = l // T
    nc = tm // T
    cos_t, sin_t, cos_n, sin_n = tabs
    gqa, gqc, gka, gkc = gains
    const2 = lambda bb, i: (0, 0)
    out_shape = (
        jax.ShapeDtypeStruct((b, nq, 128, 4 * T), BF16),
        jax.ShapeDtypeStruct((b, nq, 128, 4 * T), BF16),
        jax.ShapeDtypeStruct((b, nq, 128, 4 * T), BF16),
        jax.ShapeDtypeStruct((b, nq, 2, 128, 4 * T), BF16),
        jax.ShapeDtypeStruct((b, nq, 1, 4 * T), F32),
        jax.ShapeDtypeStruct((b, nq, 64, T), BF16),
        jax.ShapeDtypeStruct((b, nq, 256, T), BF16),
        jax.ShapeDtypeStruct((b, l // LANES, 128, LANES), BF16),
        jax.ShapeDtypeStruct((b, l, 128), BF16),
        jax.ShapeDtypeStruct((b, l, 256), BF16),
        jax.ShapeDtypeStruct((b, l, 128), BF16),
    )
    out_specs = (
        pl.BlockSpec((1, nc, 128, 4 * T), lambda bb, i: (bb, i, 0, 0)),
        pl.BlockSpec((1, nc, 128, 4 * T), lambda bb, i: (bb, i, 0, 0)),
        pl.BlockSpec((1, nc, 128, 4 * T), lambda bb, i: (bb, i, 0, 0)),
        pl.BlockSpec((1, nc, 2, 128, 4 * T), lambda bb, i: (bb, i, 0, 0, 0)),
        pl.BlockSpec((1, nc, 1, 4 * T), lambda bb, i: (bb, i, 0, 0)),
        pl.BlockSpec((1, nc, 64, T), lambda bb, i: (bb, i, 0, 0)),
        pl.BlockSpec((1, nc, 256, T), lambda bb, i: (bb, i, 0, 0)),
        pl.BlockSpec((1, tm // LANES, 128, LANES), lambda bb, i: (bb, i, 0, 0)),
        pl.BlockSpec((1, tm, 128), lambda bb, i: (bb, i, 0)),
        pl.BlockSpec((1, tm, 256), lambda bb, i: (bb, i, 0)),
        pl.BlockSpec((1, tm, 128), lambda bb, i: (bb, i, 0)),
    )
    in_specs = [
        pl.BlockSpec((1, tm, d), lambda bb, i: (bb, i, 0)),
        pl.BlockSpec((1, 6, d), lambda bb, i: (bb, 0, 0)),
        pl.BlockSpec((1, d), const2),
        pl.BlockSpec((R_TOTAL, d), const2),
        pl.BlockSpec((d, C_TOTAL), const2),
        pl.BlockSpec((32, tm), lambda bb, i: (0, i)),
        pl.BlockSpec((32, tm), lambda bb, i: (0, i)),
        pl.BlockSpec((tm, LANES), lambda bb, i: (i, 0)),
        pl.BlockSpec((tm, LANES), lambda bb, i: (i, 0)),
        pl.BlockSpec((HEAD_DIM, 1), const2),
        pl.BlockSpec((HEAD_DIM, 1), const2),
        pl.BlockSpec((1, LANES), const2),
        pl.BlockSpec((1, LANES), const2),
    ]
    return pl.pallas_call(
        functools.partial(_inproj_kernel, tile=T),
        grid=(b, l // tm),
        in_specs=in_specs,
        out_specs=out_specs,
        out_shape=out_shape,
        compiler_params=_cparams(("parallel", "arbitrary")),
        name="inproj",
    )(x, mod_l, ln, wt, wn, cos_t, sin_t, cos_n, sin_n, gqa, gqc, gka, gkc)


def _store_heads(o_ref, g_ref, heads_t):
    for p in range(len(heads_t) // 2):
        parts = []
        for o in heads_t[2 * p:2 * p + 2]:
            ms = jnp.mean(o * o, axis=0, keepdims=True)
            parts.append(o * lax.rsqrt(ms + NORM_EPS))
        ot = jnp.concatenate(parts, axis=0).T
        o_ref[0, :, 128 * p:128 * p + 128] = (ot * g_ref[:, 128 * p:128 * p + 128]).astype(BF16)


def _float_key(x):
    kb = pltpu.bitcast(x, I32)
    return jnp.where(kb < 0, INT_MIN - kb, kb)


def _key_float(k):
    return pltpu.bitcast(jnp.where(k < 0, INT_MIN - k, k), F32)


def _fold8(x, op):
    t = x.shape[0]
    return op(x.reshape(t // 8, 8, x.shape[1]), axis=0)


def _dsa_kernel(qi_ref, qa_ref, w_ref, kaki_ref, vat_ref, g_ref, o_ref,
                key_scr, s_scr, r_scr, ib_scr, m_scr, l_scr, acc_scr, *, tile, topk):
    T = tile
    K = topk
    i = pl.program_id(1)
    nk = i + 1
    rowk = lax.broadcasted_iota(I32, (T, T), 0)
    colq = lax.broadcasted_iota(I32, (T, T), 1)

    def key_chunk(j):
        return kaki_ref[0, pl.ds(pl.multiple_of(j * T, T), T), :]

    def index_dots(j):
        kc = key_chunk(j)
        for h in range(N_IDX_HEADS):
            s_scr[h] = jnp.dot(kc, qi_ref[0, 0, :, h * T:(h + 1) * T], preferred_element_type=F32)

    def attn_dots(j):
        kc = key_chunk(j)
        for h in range(N_HEADS_A):
            r_scr[h] = jnp.dot(kc, qa_ref[0, 0, :, h * T:(h + 1) * T], preferred_element_type=F32)

    def score_chunk(j, nxt, diag, mx8, mn8):
        sc = None
        for h in range(N_IDX_HEADS):
            r = jnp.maximum(s_scr[h], 0.0) * w_ref[0, 0, :, h * T:(h + 1) * T]
            sc = r if sc is None else sc + r
        index_dots(nxt)
        key = _float_key(sc)
        if diag:
            causal = rowk <= colq
            key_hi = jnp.where(causal, key, KEY_NEG_INF)
            key_lo = jnp.where(causal, key, INT_MAX)
        else:
            key_hi = key_lo = key
        key_scr[j] = key_hi
        return (jnp.maximum(mx8, _fold8(key_hi, jnp.max)),
                jnp.minimum(mn8, _fold8(key_lo, jnp.min)))

    index_dots(i)
    attn_dots(0)
    mx8, mn8 = score_chunk(i, 0, True, jnp.full((8, T), INT_MIN, I32),
                           jnp.full((8, T), INT_MAX, I32))
    mx8, mn8 = lax.fori_loop(
        0, i, lambda j, c: score_chunk(j, jnp.minimum(j + 1, i - 1), False, *c), (mx8, mn8))
    rmax = jnp.max(mx8, axis=0, keepdims=True)
    rmin = jnp.min(mn8, axis=0, keepdims=True)

    def count_ge(mid):
        def body(p, acc):
            a = _fold8(jnp.where(key_scr[2 * p] >= mid, 1, 0), jnp.sum)
            b = _fold8(jnp.where(key_scr[2 * p + 1] >= mid, 1, 0), jnp.sum)
            return acc + (a + b)
        acc = lax.fori_loop(0, (nk + 1) // 2, body, jnp.zeros((8, T), I32))
        return jnp.sum(acc, axis=0, keepdims=True)

    @pl.when(nk % 2 == 1)
    def _():
        key_scr[nk] = jnp.full((T, T), INT_MIN, I32)

    tq = i * T + lax.broadcasted_iota(I32, (1, T), 1)
    big = tq >= K
    lo0 = jnp.where(big, rmin, KEY_NEG_INF)
    hi0 = jnp.where(big, rmax + 1, KEY_NEG_INF + 1)
    clo0 = tq + 1
    chi0 = jnp.zeros((1, T), I32)

    def probe(it, lo, hi, clo, chi):
        f = ((clo - K).astype(F32) + 0.5) / jnp.maximum(clo - chi, 1).astype(F32)
        frac = jnp.where(f < 0.25, jnp.maximum(2.0 * f, 1e-3),
                         jnp.where(f > 0.75, 1.0 - jnp.maximum(2.0 - 2.0 * f, 1e-3), 0.5))
        lo_f = _key_float(lo)
        hi_f = _key_float(hi)
        fmid = _float_key(lo_f * (1.0 - frac) + hi_f * frac)
        kmid = (lo >> 1) + (hi >> 1) + (lo & hi & 1)
        inside = jnp.where(fmid > lo, jnp.where(fmid < hi, it, FLOAT_PROBES), FLOAT_PROBES)
        mid = jnp.where(inside < FLOAT_PROBES, fmid, kmid)
        mid = jnp.where(lo == 0, jnp.where(hi > 1, 1, mid), mid)
        mid = jnp.where(lo < 0, jnp.where(hi > 0, 0, mid), mid)
        c = count_ge(mid)
        ge = c >= K
        return (jnp.where(ge, mid, lo), jnp.where(ge, hi, mid),
                jnp.where(ge, c, clo), jnp.where(ge, chi, c))

    def probes(n, st):
        _, it, lo, hi, clo, chi = st
        for _ in range(n):
            lo, hi, clo, chi = probe(it, lo, hi, clo, chi)
            it = it + 1
        open_ = jnp.where(clo == K, 0.0, jnp.where((hi - 1) == lo, 0.0, 1.0))
        return jnp.sum(open_), it, lo, hi, clo, chi

    st = probes(FIRST_PROBES, (0.0, jnp.zeros((1, T), I32), lo0, hi0, clo0, chi0))
    _, _, lo, hi, clo, chi = lax.while_loop(
        lambda st: st[0] > 0.0, functools.partial(probes, PROBES_PER_CHECK), st)

    tie = jnp.where(big, jnp.where(clo > K, 1.0, 0.0), 0.0)
    thr = jnp.maximum(lo, KEY_NEG_INF + 1)
    need = (K - chi).astype(F32)
    ib_scr[...] = jnp.full((1, T), IDX_BIG, I32)

    @pl.when(jnp.sum(tie) > 0.0)
    def _():
        before = jnp.where(colq < rowk, 1.0, 0.0).astype(BF16)

        def body(j, carry):
            run, ibm8 = carry
            eq = jnp.where(key_scr[j] == thr, 1.0, 0.0)
            pc = jnp.dot(before, eq.astype(BF16), preferred_element_type=F32) + run
            idx1 = (rowk + (j * T + 1)).astype(F32)
            taken = jnp.where(pc < need, eq * idx1, 0.0)
            return (run + jnp.sum(_fold8(eq, jnp.sum), axis=0, keepdims=True),
                    jnp.maximum(ibm8, _fold8(taken, jnp.max)))

        _, ibm8 = lax.fori_loop(0, nk, body, (jnp.zeros((1, T), F32), jnp.zeros((8, T), F32)))
        ibm = jnp.max(ibm8, axis=0, keepdims=True).astype(I32)
        ib_scr[...] = jnp.where(tie > 0.0, ibm, IDX_BIG)

    ib = ib_scr[...]

    m_scr[...] = jnp.full(m_scr.shape, NEG_BIG, F32)
    l_scr[...] = jnp.zeros(l_scr.shape, F32)
    acc_scr[...] = jnp.zeros(acc_scr.shape, F32)

    def attn_body(j, carry):
        t_el = jnp.where((rowk + j * T) < ib, thr, thr + 1)
        sel = key_scr[j] >= t_el
        vt = vat_ref[0, j]
        m_new = []
        for h in range(N_HEADS_A):
            s = jnp.where(sel, r_scr[h], NEG_BIG)
            s_scr[h] = s
            mc = jnp.max(_fold8(s, jnp.max), axis=0, keepdims=True)
            m_new.append(jnp.maximum(m_scr[h], mc))
        attn_dots(jnp.minimum(j + 1, i))
        for h in range(N_HEADS_A):
            p = jnp.exp2(s_scr[h] - m_new[h])
            alpha = jnp.exp2(m_scr[h] - m_new[h])
            l_scr[h] = alpha * l_scr[h] + jnp.sum(_fold8(p, jnp.sum), axis=0, keepdims=True)
            acc_scr[h] = alpha * acc_scr[h] + jnp.dot(vt, p.astype(BF16),
                                                      preferred_element_type=F32)
            m_scr[h] = m_new[h]
        return carry

    lax.fori_loop(0, nk, attn_body, 0)
    _store_heads(o_ref, g_ref, [acc_scr[h] / l_scr[h] for h in range(N_HEADS_A)])


def _dsa(qi, qa, w, kaki, vat, g, topk):
    b, nq, _, _ = qa.shape
    T = ATT_TILE
    l = nq * T
    return pl.pallas_call(
        functools.partial(_dsa_kernel, tile=T, topk=topk),
        grid=(b, nq),
        in_specs=[
            pl.BlockSpec((1, 1, 128, 4 * T), lambda bb, i: (bb, i, 0, 0)),
            pl.BlockSpec((1, 1, 128, 4 * T), lambda bb, i: (bb, i, 0, 0)),
            pl.BlockSpec((1, 1, 1, 4 * T), lambda bb, i: (bb, i, 0, 0)),
            pl.BlockSpec((1, l, 128), lambda bb, i: (bb, 0, 0)),
            pl.BlockSpec((1, nq, 64, T), lambda bb, i: (bb, 0, 0, 0)),
            pl.BlockSpec((1, 256), lambda bb, i: (0, 0)),
        ],
        out_specs=pl.BlockSpec((1, T, 256), lambda bb, i: (bb, i, 0)),
        out_shape=jax.ShapeDtypeStruct((b, l, 256), BF16),
        scratch_shapes=[
            pltpu.VMEM((nq + nq % 2, T, T), I32),
            pltpu.VMEM((N_HEADS_A, T, T), F32),
            pltpu.VMEM((N_HEADS_A, T, T), F32),
            pltpu.VMEM((1, T), I32),
            pltpu.VMEM((N_HEADS_A, 1, T), F32),
            pltpu.VMEM((N_HEADS_A, 1, T), F32),
            pltpu.VMEM((N_HEADS_A, HEAD_DIM, T), F32),
        ],
        compiler_params=_cparams(("parallel", "arbitrary")),
        name="dsa",
    )(qi, qa, w, kaki, vat, g)


def _sb_kernel(qb_ref, kb_ref, vbt_ref, g_ref, o_ref, run_scr, acc_scr, z_scr, e_scr, c_scr, hl_scr,
               *, tile):
    T = tile
    i = pl.program_id(1)
    rowk = lax.broadcasted_iota(I32, (T, T), 0)
    colq = lax.broadcasted_iota(I32, (T, T), 1)
    strict = rowk < colq
    tmat = jnp.where(colq > rowk, 1.0, 0.0).astype(BF16)
    run_scr[...] = jnp.zeros(run_scr.shape, F32)
    acc_scr[...] = jnp.zeros(acc_scr.shape, F32)

    def scores(j):
        row0 = pl.multiple_of(j * T, T)
        for h in range(N_HEADS_B):
            p = h // 2
            kc = kb_ref[0, pl.ds(row0, T), 128 * p:128 * p + 128]
            z_scr[h] = jnp.dot(kc, qb_ref[0, 0, :, h * T:(h + 1) * T],
                               preferred_element_type=F32)

    def chunk(j, diag):
        tots = []
        for h in range(N_HEADS_B):
            z = z_scr[h]
            nz = -z
            lk = jnp.minimum(nz, 0.0) - jnp.log2(1.0 + jnp.exp2(jnp.minimum(z, nz)))
            if diag:
                lk = jnp.where(strict, lk, 0.0)
            hl_scr[h] = lk.astype(BF16)
            e_scr[h] = z + lk
            c_scr[h] = jnp.dot(tmat, hl_scr[h], preferred_element_type=F32)
            tots.append(c_scr[h][0:1, :] + lk[0:1, :])
        scores(jnp.maximum(j - 1, 0))
        for h in range(N_HEADS_B):
            a = jnp.exp2(e_scr[h] + c_scr[h] + run_scr[h])
            if diag:
                a = jnp.where(strict, a, 0.0)
            acc_scr[h] = acc_scr[h] + jnp.dot(vbt_ref[0, j, 64 * h:64 * h + 64, :], a.astype(BF16),
                                              preferred_element_type=F32)
            run_scr[h] = run_scr[h] + tots[h]

    scores(i)
    chunk(i, True)

    def body(jj, carry):
        chunk(i - 1 - jj, False)
        return carry

    lax.fori_loop(0, i, body, 0)
    _store_heads(o_ref, g_ref, [acc_scr[h] for h in range(N_HEADS_B)])


def _sb(qb, kb, vbt, g):
    b, nq, _, _ = qb.shape
    T = ATT_TILE
    l = nq * T
    return pl.pallas_call(
        functools.partial(_sb_kernel, tile=T),
        grid=(b, nq),
        in_specs=[
            pl.BlockSpec((1, 1, 128, 4 * T), lambda bb, i: (bb, i, 0, 0)),
            pl.BlockSpec((1, l, 256), lambda bb, i: (bb, 0, 0)),
            pl.BlockSpec((1, nq, 256, T), lambda bb, i: (bb, 0, 0, 0)),
            pl.BlockSpec((1, 256), lambda bb, i: (0, 0)),
        ],
        out_specs=pl.BlockSpec((1, T, 256), lambda bb, i: (bb, i, 0)),
        out_shape=jax.ShapeDtypeStruct((b, l, 256), BF16),
        scratch_shapes=[
            pltpu.VMEM((N_HEADS_B, 1, T), F32),
            pltpu.VMEM((N_HEADS_B, HEAD_DIM, T), F32),
            pltpu.VMEM((N_HEADS_B, T, T), F32),
            pltpu.VMEM((N_HEADS_B, T, T), F32),
            pltpu.VMEM((N_HEADS_B, T, T), F32),
            pltpu.VMEM((N_HEADS_B, T, T), BF16),
        ],
        compiler_params=_cparams(("parallel", "arbitrary")),
        name="stickbreak",
    )(qb, kb, vbt, g)


def _swa_kernel(sink_ref, qc_ref, kc_ref, vct_ref, g_ref, o_ref, s_scr, *, tile):
    T = tile
    NKEY = T + WINDOW
    i = pl.program_id(1)
    start = pl.multiple_of(jnp.maximum(i * T - WINDOW, 0), WINDOW)
    kidx = start + lax.broadcasted_iota(I32, (NKEY, T), 0)
    tq = i * T + lax.broadcasted_iota(I32, (NKEY, T), 1)
    d = tq - kidx
    band = (d >= 0) & (d < WINDOW)
    kc = kc_ref[0, pl.ds(start, NKEY), :]
    c0 = start // WINDOW
    per_kv = N_HEADS_C // N_KV_C
    for h in range(N_HEADS_C):
        s_scr[h] = jnp.dot(kc, qc_ref[0, 0, h // per_kv, :, (h % per_kv) * T:(h % per_kv + 1) * T],
                           preferred_element_type=F32)
    outs = []
    for h in range(N_HEADS_C):
        g = h // per_kv
        s = jnp.where(band, s_scr[h], NEG_BIG)
        sink = sink_ref[h] * LOG2E
        m = jnp.maximum(jnp.max(_fold8(s, jnp.max), axis=0, keepdims=True), sink)
        e = jnp.exp2(s - m)
        den = jnp.sum(_fold8(e, jnp.sum), axis=0, keepdims=True) + jnp.exp2(sink - m)
        p = e.astype(BF16)
        o = None
        for c in range(NKEY // WINDOW):
            part = jnp.dot(vct_ref[0, c0 + c, 64 * g:64 * g + 64, :],
                           p[c * WINDOW:(c + 1) * WINDOW, :], preferred_element_type=F32)
            o = part if o is None else o + part
        outs.append(o / den)
    _store_heads(o_ref, g_ref, outs)


def _swa(sinks, qc, kc, vct, g):
    b, nq = qc.shape[:2]
    T = ATT_TILE
    l = nq * T
    return pl.pallas_call(
        functools.partial(_swa_kernel, tile=T),
        grid=(b, nq),
        in_specs=[
            pl.BlockSpec(memory_space=pltpu.SMEM),
            pl.BlockSpec((1, 1, 2, 128, 4 * T), lambda bb, i: (bb, i, 0, 0, 0)),
            pl.BlockSpec((1, l, 128), lambda bb, i: (bb, 0, 0)),
            pl.BlockSpec((1, l // LANES, 128, LANES), lambda bb, i: (bb, 0, 0, 0)),
            pl.BlockSpec((1, 512), lambda bb, i: (0, 0)),
        ],
        out_specs=pl.BlockSpec((1, T, 512), lambda bb, i: (bb, i, 0)),
        out_shape=jax.ShapeDtypeStruct((b, l, 512), BF16),
        scratch_shapes=[pltpu.VMEM((N_HEADS_C, T + WINDOW, T), F32)],
        compiler_params=_cparams(("parallel", "arbitrary")),
        name="swa",
    )(sinks, qc, kc, vct, g)


def _ffn_chunks(d_ff):
    chunks, c0 = [], 0
    while c0 < d_ff:
        cw = min(1024, d_ff - c0)
        chunks.append((c0, cw))
        c0 += cw
    return chunks


def _merge_ffn_kernel(x_ref, oa_ref, ob_ref, oc_ref, mod_ref, ln_ref, woa_ref, wob_ref, woc_ref,
                      wg_ref, wu_ref, wd_ref, out_ref):
    x = x_ref[0]
    y = (jnp.dot(oa_ref[0], woa_ref[...], preferred_element_type=F32)
         + jnp.dot(ob_ref[0], wob_ref[...], preferred_element_type=F32)
         + jnp.dot(oc_ref[0], woc_ref[...], preferred_element_type=F32))
    x1 = x + mod_ref[0, 2:3, :] * y
    ms = jnp.mean(x1 * x1, axis=-1, keepdims=True)
    h = x1 * lax.rsqrt(ms + NORM_EPS) * ln_ref[...]
    hb = (h * (1.0 + mod_ref[0, 4:5, :]) + mod_ref[0, 3:4, :]).astype(BF16)
    acc = None
    for c0, cw in _ffn_chunks(wg_ref.shape[1]):
        gt = jnp.dot(hb, wg_ref[:, c0:c0 + cw], preferred_element_type=F32)
        up = jnp.dot(hb, wu_ref[:, c0:c0 + cw], preferred_element_type=F32)
        act = (gt * jax.nn.sigmoid(gt) * up).astype(BF16)
        part = jnp.dot(act, wd_ref[c0:c0 + cw, :], preferred_element_type=F32)
        acc = part if acc is None else acc + part
    out_ref[0] = x1 + mod_ref[0, 5:6, :] * acc


def _merge_ffn(x, oa, ob, oc, mod_l, ln, woa, wob, woc, wg, wu, wd):
    b, l, d = x.shape
    tm = TOK_TILE
    dff = wg.shape[1]
    const2 = lambda bb, i: (0, 0)
    once = pl.Buffered(1)
    return pl.pallas_call(
        _merge_ffn_kernel,
        grid=(b, l // tm),
        in_specs=[
            pl.BlockSpec((1, tm, d), lambda bb, i: (bb, i, 0)),
            pl.BlockSpec((1, tm, 256), lambda bb, i: (bb, i, 0)),
            pl.BlockSpec((1, tm, 256), lambda bb, i: (bb, i, 0)),
            pl.BlockSpec((1, tm, 512), lambda bb, i: (bb, i, 0)),
            pl.BlockSpec((1, 6, d), lambda bb, i: (bb, 0, 0)),
            pl.BlockSpec((1, d), const2),
            pl.BlockSpec((256, d), const2, pipeline_mode=once),
            pl.BlockSpec((256, d), const2, pipeline_mode=once),
            pl.BlockSpec((512, d), const2, pipeline_mode=once),
            pl.BlockSpec((d, dff), const2, pipeline_mode=once),
            pl.BlockSpec((d, dff), const2, pipeline_mode=once),
            pl.BlockSpec((dff, d), const2, pipeline_mode=once),
        ],
        out_specs=pl.BlockSpec((1, tm, d), lambda bb, i: (bb, i, 0)),
        out_shape=jax.ShapeDtypeStruct((b, l, d), F32),
        compiler_params=_cparams(("parallel", "arbitrary")),
        name="merge_ffn",
    )(x, oa, ob, oc, mod_l, ln, woa, wob, woc, wg, wu, wd)


def _rope_tables(l):
    inv = 1.0 / (ROPE_THETA ** (jnp.arange(0, HEAD_DIM, 2, dtype=F32) / HEAD_DIM))
    ang = jnp.arange(l, dtype=F32)[:, None] * inv[None, :]
    cos, sin = jnp.cos(ang), jnp.sin(ang)
    cos_n = jnp.tile(cos, (1, 4))
    sin_n = jnp.concatenate([-sin, sin, -sin, sin], axis=1)
    return cos.T, sin.T, cos_n, sin_n


def _split_w_in(w_in):
    widths = (256, 64, 64, 256, 64, 4, 256, 256, 256, 512, 128, 128)
    offs = [0]
    for w in widths:
        offs.append(offs[-1] + w)
    qa, ka, va, qi, ki, wi, qb, kb, vb, qc, kc, vc = [w_in[:, :, offs[k]:offs[k + 1]]
                                                      for k in range(len(widths))]
    idx_scale = float((N_IDX_HEADS * HEAD_DIM) ** -0.5)
    att_scale = float(HEAD_DIM ** -0.5)
    pad = jnp.zeros(wi.shape[:2] + (R_TOTAL - R_W - N_IDX_HEADS,), w_in.dtype)
    wt = jnp.concatenate([qa, qi, qb * (att_scale * LOG2E), qc, va, vb, vc, wi * idx_scale, pad],
                         axis=2)
    wt = jnp.swapaxes(wt, 1, 2).astype(BF16)
    wn = jnp.concatenate([ka, ki, kb, kc], axis=2).astype(BF16)
    return wt, wn


@jax.jit
def kernel(x, c, ln1, ln2, w_mod, b_mod, w_in, qn_a, kn_a, qn_c, kn_c, sinks, g_out, w_o,
           w_gate, w_up, w_down):
    depth = w_in.shape[0]
    b, l, d = x.shape
    assert l % TOK_TILE == 0 and l >= ATT_TILE + WINDOW
    topk = min(TOPK_MAX, l // 4)
    att_scale = float(HEAD_DIM ** -0.5)

    tabs = _rope_tables(l)
    mod = _modulation(c, w_mod, b_mod)
    wt, wn = _split_w_in(w_in)
    w_o_b = w_o.astype(BF16)
    w_g_b, w_u_b, w_d_b = w_gate.astype(BF16), w_up.astype(BF16), w_down.astype(BF16)
    ones = jnp.ones((HEAD_DIM,), F32)

    for li in range(depth):
        gains = ((qn_a[li] * (att_scale * LOG2E)).reshape(HEAD_DIM, 1),
                 (qn_c[li] * (att_scale * LOG2E)).reshape(HEAD_DIM, 1),
                 jnp.concatenate([kn_a[li], ones]).reshape(1, LANES),
                 jnp.concatenate([kn_c[li], kn_c[li]]).reshape(1, LANES))
        (qa, qi, qb, qc, w, vat, vbt, vct, kaki, kb, kc) = _inproj(
            x, mod[li], ln1[li].reshape(1, d), wt[li], wn[li], tabs, gains)
        g = g_out[li].reshape(1, -1)
        oa = _dsa(qi, qa, w, kaki, vat, g[:, 0:256], topk)
        ob = _sb(qb, kb, vbt, g[:, 256:512])
        oc = _swa(sinks[li], qc, kc, vct, g[:, 512:1024])
        x = _merge_ffn(x, oa, ob, oc, mod[li], ln2[li].reshape(1, d),
                       w_o_b[li, 0:256], w_o_b[li, 256:512], w_o_b[li, 512:1024],
                       w_g_b[li], w_u_b[li], w_d_b[li])
    return x
```

```python
import functools

import jax
import jax.numpy as jnp
from jax import lax
from jax.experimental import pallas as pl
from jax.experimental.pallas import tpu as pltpu

F32 = jnp.float32
BF16 = jnp.bfloat16
I32 = jnp.int32

HEAD_DIM = 64
N_HEADS_A = 4
N_HEADS_B = 4
N_HEADS_C = 8
N_KV_C = 2
N_IDX_HEADS = 4
TOPK_MAX = 256
WINDOW = 128
ROPE_THETA = 10000.0
NORM_EPS = 1e-6
LANES = 128
ATT_TILE = 256
TOK_TILE = 512
VMEM_LIMIT = 48 * 1024 * 1024

NEG_BIG = -1e30
INT_MIN = -2 ** 31
INT_MAX = 2 ** 31 - 1
KEY_NEG_INF = -0x7F800000
IDX_BIG = 2 ** 30
LOG2E = 1.4426950408889634
FIRST_PROBES = 14
PROBES_PER_CHECK = 2
FLOAT_PROBES = 30

R_QA, R_QI, R_QB, R_QC = 0, 256, 512, 768
R_VA, R_VB, R_VC, R_W = 1280, 1344, 1600, 1728
R_TOTAL = 1744
C_KAKI, C_KB, C_KC = 0, 128, 384
C_TOTAL = 512


def _cparams(sem):
    return pltpu.CompilerParams(dimension_semantics=sem, vmem_limit_bytes=VMEM_LIMIT)


def _mod_kernel(c_ref, w_ref, b_ref, o_ref):
    c = c_ref[...]
    ca = (c * jax.nn.sigmoid(c)).astype(BF16)
    y = jnp.dot(ca, w_ref[0].astype(BF16), preferred_element_type=F32)
    o_ref[0, 0] = y + b_ref[0, 0]


def _modulation(c, w_mod, b_mod):
    depth, d, _ = w_mod.shape
    b = c.shape[0]
    out = pl.pallas_call(
        _mod_kernel,
        grid=(depth, 6),
        in_specs=[
            pl.BlockSpec((b, d), lambda l, k: (0, 0)),
            pl.BlockSpec((1, d, d), lambda l, k: (l, 0, k)),
            pl.BlockSpec((1, 1, 1, d), lambda l, k: (l, k, 0, 0)),
        ],
        out_specs=pl.BlockSpec((1, 1, b, d), lambda l, k: (l, k, 0, 0)),
        out_shape=jax.ShapeDtypeStruct((depth, 6, b, d), F32),
        compiler_params=_cparams(("arbitrary", "arbitrary")),
        name="modulation",
    )(c, w_mod, b_mod.reshape(depth, 6, 1, d))
    return jnp.transpose(out, (0, 2, 1, 3))


def _inproj_kernel(x_ref, mod_ref, ln_ref, wt_ref, wn_ref, cost_ref, sint_ref, cosn_ref, sinn_ref,
                   gqa_ref, gqc_ref, gka_ref, gkc_ref,
                   qa_ref, qi_ref, qb_ref, qc_ref, w_ref, vat_ref, vbt_ref, vct_ref,
                   kaki_ref, kb_ref, kc_ref, *, tile):
    T = tile
    nc = x_ref.shape[1] // T
    gqa = gqa_ref[...]
    gqc = gqc_ref[...]
    lane = lax.broadcasted_iota(I32, (T, LANES), 1)
    left = lane < HEAD_DIM
    first = (lane & 32) == 0
    zero_half = jnp.zeros((HEAD_DIM, T), BF16)
    per_kv = N_HEADS_C // N_KV_C

    for c in range(nc):
        tok = slice(c * T, (c + 1) * T)
        x = x_ref[0, tok, :]
        ms = jnp.mean(x * x, axis=-1, keepdims=True)
        h = x * lax.rsqrt(ms + NORM_EPS) * ln_ref[...]
        h = h * (1.0 + mod_ref[0, 1:2, :]) + mod_ref[0, 0:1, :]
        hb = h.astype(BF16)
        pt = lax.dot_general(wt_ref[...], hb, (((1,), (1,)), ((), ())),
                             preferred_element_type=F32)
        pn = jnp.dot(hb, wn_ref[...], preferred_element_type=F32)

        cos_t = cost_ref[:, tok]
        sin_t = sint_ref[:, tok]

        def rope_t(y):
            y1, y2 = y[0:32], y[32:64]
            return jnp.concatenate([y1 * cos_t - y2 * sin_t, y2 * cos_t + y1 * sin_t], axis=0)

        def rms_t(y, g):
            m = jnp.mean(y * y, axis=0, keepdims=True)
            return y * lax.rsqrt(m + NORM_EPS) * g

        def put_padded(ref, lead, y, half, col):
            idx = (0, c) + lead
            ref[idx + (slice(64 * half, 64 * half + 64), slice(col * T, (col + 1) * T))] = \
                y.astype(BF16)
            ref[idx + (slice(64 * (1 - half), 64 * (1 - half) + 64),
                       slice(col * T, (col + 1) * T))] = zero_half

        for hh in range(N_HEADS_A):
            put_padded(qa_ref, (), rope_t(rms_t(pt[R_QA + 64 * hh:R_QA + 64 * hh + 64], gqa)), 0, hh)
        for hh in range(N_IDX_HEADS):
            put_padded(qi_ref, (), rope_t(pt[R_QI + 64 * hh:R_QI + 64 * hh + 64]), 1, hh)
        for hh in range(N_HEADS_B):
            put_padded(qb_ref, (), pt[R_QB + 64 * hh:R_QB + 64 * hh + 64], hh % 2, hh)
        for hh in range(N_HEADS_C):
            g = hh // per_kv
            put_padded(qc_ref, (g,), rope_t(rms_t(pt[R_QC + 64 * hh:R_QC + 64 * hh + 64], gqc)),
                       g, hh % per_kv)
        vat_ref[0, c] = pt[R_VA:R_VA + 64].astype(BF16)
        vbt_ref[0, c] = pt[R_VB:R_VB + 256].astype(BF16)
        for hh in range(N_IDX_HEADS):
            w_ref[0, c, :, hh * T:(hh + 1) * T] = pt[R_W + hh:R_W + hh + 1]
        for cc in range(T // LANES):
            vct_ref[0, c * (T // LANES) + cc] = \
                pt[R_VC:R_VC + 128, cc * LANES:(cc + 1) * LANES].astype(BF16)

        cos_n = cosn_ref[tok, :]
        sin_n = sinn_ref[tok, :]

        def rope_n(y):
            rot = jnp.where(first, pltpu.roll(y, 96, 1), pltpu.roll(y, 32, 1))
            return y * cos_n + rot * sin_n

        def head_rsqrt(y):
            sq = y * y
            s0 = jnp.sum(jnp.where(left, sq, 0.0), axis=-1, keepdims=True)
            s1 = jnp.sum(jnp.where(left, 0.0, sq), axis=-1, keepdims=True)
            return lax.rsqrt(jnp.where(left, s0, s1) * (1.0 / HEAD_DIM) + NORM_EPS)

        y = pn[:, C_KAKI:C_KAKI + 128]
        fac = jnp.where(left, head_rsqrt(y), 1.0) * gka_ref[...]
        kaki_ref[0, tok, :] = rope_n(y * fac).astype(BF16)
        kb_ref[0, tok, :] = pn[:, C_KB:C_KB + 256].astype(BF16)
        y = pn[:, C_KC:C_KC + 128]
        kc_ref[0, tok, :] = rope_n(y * head_rsqrt(y) * gkc_ref[...]).astype(BF16)


def _inproj(x, mod_l, ln, wt, wn, tabs, gains):
    b, l, d = x.shape
    T = ATT_TILE
    tm = TOK_TILE
    nq = l // T
    nc = tm // T
    cos_t, sin_t, cos_n, sin_n = tabs
    gqa, gqc, gka, gkc = gains
    const2 = lambda bb, i: (0, 0)
    out_shape = (
        jax.ShapeDtypeStruct((b, nq, 128, 4 * T), BF16),
        jax.ShapeDtypeStruct((b, nq, 128, 4 * T), BF16),
        jax.ShapeDtypeStruct((b, nq, 128, 4 * T), BF16),
        jax.ShapeDtypeStruct((b, nq, 2, 128, 4 * T), BF16),
        jax.ShapeDtypeStruct((b, nq, 1, 4 * T), F32),
        jax.ShapeDtypeStruct((b, nq, 64, T), BF16),
        jax.ShapeDtypeStruct((b, nq, 256, T), BF16),
        jax.ShapeDtypeStruct((b, l // LANES, 128, LANES), BF16),
        jax.ShapeDtypeStruct((b, l, 128), BF16),
        jax.ShapeDtypeStruct((b, l, 256), BF16),
        jax.ShapeDtypeStruct((b, l, 128), BF16),
    )
    out_specs = (
        pl.BlockSpec((1, nc, 128, 4 * T), lambda bb, i: (bb, i, 0, 0)),
        pl.BlockSpec((1, nc, 128, 4 * T), lambda bb, i: (bb, i, 0, 0)),
        pl.BlockSpec((1, nc, 128, 4 * T), lambda bb, i: (bb, i, 0, 0)),
        pl.BlockSpec((1, nc, 2, 128, 4 * T), lambda bb, i: (bb, i, 0, 0, 0)),
        pl.BlockSpec((1, nc, 1, 4 * T), lambda bb, i: (bb, i, 0, 0)),
        pl.BlockSpec((1, nc, 64, T), lambda bb, i: (bb, i, 0, 0)),
        pl.BlockSpec((1, nc, 256, T), lambda bb, i: (bb, i, 0, 0)),
        pl.BlockSpec((1, tm // LANES, 128, LANES), lambda bb, i: (bb, i, 0, 0)),
        pl.BlockSpec((1, tm, 128), lambda bb, i: (bb, i, 0)),
        pl.BlockSpec((1, tm, 256), lambda bb, i: (bb, i, 0)),
        pl.BlockSpec((1, tm, 128), lambda bb, i: (bb, i, 0)),
    )
    in_specs = [
        pl.BlockSpec((1, tm, d), lambda bb, i: (bb, i, 0)),
        pl.BlockSpec((1, 6, d), lambda bb, i: (bb, 0, 0)),
        pl.BlockSpec((1, d), const2),
        pl.BlockSpec((R_TOTAL, d), const2),
        pl.BlockSpec((d, C_TOTAL), const2),
        pl.BlockSpec((32, tm), lambda bb, i: (0, i)),
        pl.BlockSpec((32, tm), lambda bb, i: (0, i)),
        pl.BlockSpec((tm, LANES), lambda bb, i: (i, 0)),
        pl.BlockSpec((tm, LANES), lambda bb, i: (i, 0)),
        pl.BlockSpec((HEAD_DIM, 1), const2),
        pl.BlockSpec((HEAD_DIM, 1), const2),
        pl.BlockSpec((1, LANES), const2),
        pl.BlockSpec((1, LANES), const2),
    ]
    return pl.pallas_call(
        functools.partial(_inproj_kernel, tile=T),
        grid=(b, l // tm),
        in_specs=in_specs,
        out_specs=out_specs,
        out_shape=out_shape,
        compiler_params=_cparams(("parallel", "arbitrary")),
        name="inproj",
    )(x, mod_l, ln, wt, wn, cos_t, sin_t, cos_n, sin_n, gqa, gqc, gka, gkc)


def _store_heads(o_ref, g_ref, heads_t):
    for p in range(len(heads_t) // 2):
        parts = []
        for o in heads_t[2 * p:2 * p + 2]:
            ms = jnp.mean(o * o, axis=0, keepdims=True)
            parts.append(o * lax.rsqrt(ms + NORM_EPS))
        ot = jnp.concatenate(parts, axis=0).T
        o_ref[0, :, 128 * p:128 * p + 128] = (ot * g_ref[:, 128 * p:128 * p + 128]).astype(BF16)


def _float_key(x):
    kb = pltpu.bitcast(x, I32)
    return jnp.where(kb < 0, INT_MIN - kb, kb)


def _key_float(k):
    return pltpu.bitcast(jnp.where(k < 0, INT_MIN - k, k), F32)


def _fold16(x, op):
    t = x.shape[0]
    return op(x.reshape(t // 16, 16, x.shape[1]), axis=0)


def _fold8(x, op):
    t = x.shape[0]
    return op(x.reshape(t // 8, 8, x.shape[1]), axis=0)


def _dsa_kernel(qi_ref, qa_ref, w_ref, kaki_ref, vat_ref, g_ref, o_ref,
                key_scr, s_scr, r_scr, p_scr, ib_scr, m_scr, l_scr, acc_scr, *, tile, topk):
    T = tile
    K = topk
    i = pl.program_id(1)
    nk = i + 1
    rowk = lax.broadcasted_iota(I32, (T, T), 0)
    colq = lax.broadcasted_iota(I32, (T, T), 1)
    negk = -rowk

    def key_chunk(j):
        return kaki_ref[0, pl.ds(pl.multiple_of(j * T, T), T), :]

    def index_dots(j):
        kc = key_chunk(j)
        for h in range(N_IDX_HEADS):
            s_scr[h] = jnp.dot(kc, qi_ref[0, 0, :, h * T:(h + 1) * T], preferred_element_type=F32)

    def attn_dots(j):
        kc = key_chunk(j)
        for h in range(N_HEADS_A):
            r_scr[h] = jnp.dot(kc, qa_ref[0, 0, :, h * T:(h + 1) * T], preferred_element_type=F32)

    def score_chunk(j, nxt, diag, mx8, mn8):
        sc = None
        for h in range(N_IDX_HEADS):
            r = jnp.maximum(s_scr[h], 0.0) * w_ref[0, 0, :, h * T:(h + 1) * T]
            sc = r if sc is None else sc + r
        index_dots(nxt)
        key = jnp.where(sc == 0.0, negk - j * T, _float_key(sc))
        if diag:
            causal = rowk <= colq
            key_hi = jnp.where(causal, key, KEY_NEG_INF)
            key_lo = jnp.where(causal, key, INT_MAX)
        else:
            key_hi = key_lo = key
        key_scr[j] = key_hi
        return (jnp.maximum(mx8, _fold8(key_hi, jnp.max)),
                jnp.minimum(mn8, _fold8(key_lo, jnp.min)))

    index_dots(i)
    attn_dots(0)
    mx8, mn8 = score_chunk(i, 0, True, jnp.full((8, T), INT_MIN, I32),
                           jnp.full((8, T), INT_MAX, I32))
    mx8, mn8 = lax.fori_loop(
        0, i, lambda j, c: score_chunk(j, jnp.minimum(j + 1, i - 1), False, *c), (mx8, mn8))
    rmax = jnp.max(mx8, axis=0, keepdims=True)
    rmin = jnp.min(mn8, axis=0, keepdims=True)

    def count_ge(mid):
        def body(p, acc):
            a = _fold8(jnp.where(key_scr[2 * p] >= mid, 1, 0), jnp.sum)
            b = _fold8(jnp.where(key_scr[2 * p + 1] >= mid, 1, 0), jnp.sum)
            return acc + (a + b)
        acc = lax.fori_loop(0, (nk + 1) // 2, body, jnp.zeros((8, T), I32))
        return jnp.sum(acc, axis=0, keepdims=True)

    @pl.when(nk % 2 == 1)
    def _():
        key_scr[nk] = jnp.full((T, T), INT_MIN, I32)

    tq = i * T + lax.broadcasted_iota(I32, (1, T), 1)
    big = tq >= K
    lo0 = jnp.where(big, rmin, KEY_NEG_INF)
    hi0 = jnp.where(big, rmax + 1, KEY_NEG_INF + 1)
    clo0 = tq + 1
    chi0 = jnp.zeros((1, T), I32)

    zero_lo = -nk * T

    def probe(it, lo, hi, clo, chi):
        f = ((clo - K).astype(F32) + 0.5) / jnp.maximum(clo - chi, 1).astype(F32)
        frac = jnp.where(f < 0.25, jnp.maximum(2.0 * f, 1e-3),
                         jnp.where(f > 0.75, 1.0 - jnp.maximum(2.0 - 2.0 * f, 1e-3), 0.5))
        lo_f = _key_float(lo)
        hi_f = _key_float(hi)
        fmid = _float_key(lo_f * (1.0 - frac) + hi_f * frac)
        kmid = (lo >> 1) + (hi >> 1) + (lo & hi & 1)
        inside = jnp.where(fmid > lo, jnp.where(fmid < hi, it, FLOAT_PROBES), FLOAT_PROBES)
        mid = jnp.where(inside < FLOAT_PROBES, fmid, kmid)
        mid = jnp.where(lo == 0, jnp.where(hi > 1, 1, mid), mid)
        mid = jnp.where(lo < zero_lo, jnp.where(hi > zero_lo, zero_lo, mid), mid)
        mid = jnp.where(lo < 0, jnp.where(hi > 0, 0, mid), mid)
        c = count_ge(mid)
        ge = c >= K
        return (jnp.where(ge, mid, lo), jnp.where(ge, hi, mid),
                jnp.where(ge, c, clo), jnp.where(ge, chi, c))

    def probes(n, st):
        _, it, lo, hi, clo, chi = st
        for _ in range(n):
            lo, hi, clo, chi = probe(it, lo, hi, clo, chi)
            it = it + 1
        open_ = jnp.where(clo == K, 0.0, jnp.where((hi - 1) == lo, 0.0, 1.0))
        return jnp.sum(open_), it, lo, hi, clo, chi

    st = probes(FIRST_PROBES, (0.0, jnp.zeros((1, T), I32), lo0, hi0, clo0, chi0))
    _, _, lo, hi, clo, chi = lax.while_loop(
        lambda st: st[0] > 0.0, functools.partial(probes, PROBES_PER_CHECK), st)

    tie = jnp.where(big, jnp.where(clo > K, 1.0, 0.0), 0.0)
    thr = jnp.maximum(lo, KEY_NEG_INF + 1)
    need = (K - chi).astype(F32)
    ib_scr[...] = jnp.full((1, T), IDX_BIG, I32)

    @pl.when(jnp.sum(tie) > 0.0)
    def _():
        before = jnp.where(colq < rowk, 1.0, 0.0).astype(BF16)

        def body(j, carry):
            run, ibm8 = carry
            eq = jnp.where(key_scr[j] == thr, 1.0, 0.0)
            pc = jnp.dot(before, eq.astype(BF16), preferred_element_type=F32) + run
            idx1 = (rowk + (j * T + 1)).astype(F32)
            taken = jnp.where(pc < need, eq * idx1, 0.0)
            return (run + jnp.sum(_fold8(eq, jnp.sum), axis=0, keepdims=True),
                    jnp.maximum(ibm8, _fold8(taken, jnp.max)))

        _, ibm8 = lax.fori_loop(0, nk, body, (jnp.zeros((1, T), F32), jnp.zeros((8, T), F32)))
        ibm = jnp.max(ibm8, axis=0, keepdims=True).astype(I32)
        ib_scr[...] = jnp.where(tie > 0.0, ibm, IDX_BIG)

    ib = ib_scr[...]

    m_scr[...] = jnp.full(m_scr.shape, NEG_BIG, F32)
    l_scr[...] = jnp.zeros(l_scr.shape, F32)
    acc_scr[...] = jnp.zeros(acc_scr.shape, F32)

    def attn_body(j, carry):
        t_el = jnp.where((rowk + j * T) < ib, thr, thr + 1)
        neg = jnp.where(key_scr[j] >= t_el, 0.0, NEG_BIG).astype(BF16)
        m_new = []
        for h in range(N_HEADS_A):
            sb = r_scr[h].astype(BF16) + neg
            mc = jnp.max(_fold16(sb, jnp.max), axis=0, keepdims=True).astype(F32)
            m_new.append(jnp.maximum(m_scr[h], mc))
        for h in range(N_HEADS_A):
            p = jnp.exp2((r_scr[h] - m_new[h]).astype(BF16) + neg)
            p_scr[h] = p
        attn_dots(jnp.minimum(j + 1, i))
        vt1 = jnp.concatenate([vat_ref[0, j], jnp.ones((16, T), BF16)], axis=0)
        for h in range(N_HEADS_A):
            alpha = jnp.exp2(m_scr[h] - m_new[h])
            pv = jnp.dot(vt1, p_scr[h], preferred_element_type=F32)
            l_scr[h] = alpha * l_scr[h] + pv[HEAD_DIM:HEAD_DIM + 1]
            acc_scr[h] = alpha * acc_scr[h] + pv[0:HEAD_DIM]
            m_scr[h] = m_new[h]
        return carry

    lax.fori_loop(0, nk, attn_body, 0)
    _store_heads(o_ref, g_ref, [acc_scr[h] / l_scr[h] for h in range(N_HEADS_A)])


def _dsa(qi, qa, w, kaki, vat, g, topk):
    b, nq, _, _ = qa.shape
    T = ATT_TILE
    l = nq * T
    return pl.pallas_call(
        functools.partial(_dsa_kernel, tile=T, topk=topk),
        grid=(b, nq),
        in_specs=[
            pl.BlockSpec((1, 1, 128, 4 * T), lambda bb, i: (bb, i, 0, 0)),
            pl.BlockSpec((1, 1, 128, 4 * T), lambda bb, i: (bb, i, 0, 0)),
            pl.BlockSpec((1, 1, 1, 4 * T), lambda bb, i: (bb, i, 0, 0)),
            pl.BlockSpec((1, l, 128), lambda bb, i: (bb, 0, 0)),
            pl.BlockSpec((1, nq, 64, T), lambda bb, i: (bb, 0, 0, 0)),
            pl.BlockSpec((1, 256), lambda bb, i: (0, 0)),
        ],
        out_specs=pl.BlockSpec((1, T, 256), lambda bb, i: (bb, i, 0)),
        out_shape=jax.ShapeDtypeStruct((b, l, 256), BF16),
        scratch_shapes=[
            pltpu.VMEM((nq + nq % 2, T, T), I32),
            pltpu.VMEM((N_HEADS_A, T, T), F32),
            pltpu.VMEM((N_HEADS_A, T, T), F32),
            pltpu.VMEM((N_HEADS_A, T, T), BF16),
            pltpu.VMEM((1, T), I32),
            pltpu.VMEM((N_HEADS_A, 1, T), F32),
            pltpu.VMEM((N_HEADS_A, 1, T), F32),
            pltpu.VMEM((N_HEADS_A, HEAD_DIM, T), F32),
        ],
        compiler_params=_cparams(("parallel", "arbitrary")),
        name="dsa",
    )(qi, qa, w, kaki, vat, g)


def _sb_kernel(qb_ref, kb_ref, vbt_ref, g_ref, o_ref, run_scr, acc_scr, z_scr, e_scr, c_scr, hl_scr,
               *, tile):
    T = tile
    i = pl.program_id(1)
    rowk = lax.broadcasted_iota(I32, (T, T), 0)
    colq = lax.broadcasted_iota(I32, (T, T), 1)
    strict = rowk < colq
    tmat = jnp.where(colq > rowk, 1.0, 0.0).astype(BF16)
    run_scr[...] = jnp.zeros(run_scr.shape, F32)
    acc_scr[...] = jnp.zeros(acc_scr.shape, F32)

    def scores(j):
        row0 = pl.multiple_of(j * T, T)
        for h in range(N_HEADS_B):
            p = h // 2
            kc = kb_ref[0, pl.ds(row0, T), 128 * p:128 * p + 128]
            z_scr[h] = jnp.dot(kc, qb_ref[0, 0, :, h * T:(h + 1) * T],
                               preferred_element_type=F32)

    def chunk(j, diag):
        tots = []
        for h in range(N_HEADS_B):
            z = z_scr[h]
            nz = -z
            lk = jnp.minimum(nz, 0.0) - jnp.log2(1.0 + jnp.exp2(jnp.minimum(z, nz)))
            if diag:
                lk = jnp.where(strict, lk, 0.0)
            hl_scr[h] = lk.astype(BF16)
            e_scr[h] = z + lk
            c_scr[h] = jnp.dot(tmat, hl_scr[h], preferred_element_type=F32)
            tots.append(c_scr[h][0:1, :] + lk[0:1, :])
        scores(jnp.maximum(j - 1, 0))
        for h in range(N_HEADS_B):
            a = jnp.exp2(e_scr[h] + c_scr[h] + run_scr[h])
            if diag:
                a = jnp.where(strict, a, 0.0)
            acc_scr[h] = acc_scr[h] + jnp.dot(vbt_ref[0, j, 64 * h:64 * h + 64, :], a.astype(BF16),
                                              preferred_element_type=F32)
            run_scr[h] = run_scr[h] + tots[h]

    scores(i)
    chunk(i, True)

    def body(jj, carry):
        chunk(i - 1 - jj, False)
        return carry

    lax.fori_loop(0, i, body, 0)
    _store_heads(o_ref, g_ref, [acc_scr[h] for h in range(N_HEADS_B)])


def _sb(qb, kb, vbt, g):
    b, nq, _, _ = qb.shape
    T = ATT_TILE
    l = nq * T
    return pl.pallas_call(
        functools.partial(_sb_kernel, tile=T),
        grid=(b, nq),
        in_specs=[
            pl.BlockSpec((1, 1, 128, 4 * T), lambda bb, i: (bb, i, 0, 0)),
            pl.BlockSpec((1, l, 256), lambda bb, i: (bb, 0, 0)),
            pl.BlockSpec((1, nq, 256, T), lambda bb, i: (bb, 0, 0, 0)),
            pl.BlockSpec((1, 256), lambda bb, i: (0, 0)),
        ],
        out_specs=pl.BlockSpec((1, T, 256), lambda bb, i: (bb, i, 0)),
        out_shape=jax.ShapeDtypeStruct((b, l, 256), BF16),
        scratch_shapes=[
            pltpu.VMEM((N_HEADS_B, 1, T), F32),
            pltpu.VMEM((N_HEADS_B, HEAD_DIM, T), F32),
            pltpu.VMEM((N_HEADS_B, T, T), F32),
            pltpu.VMEM((N_HEADS_B, T, T), F32),
            pltpu.VMEM((N_HEADS_B, T, T), F32),
            pltpu.VMEM((N_HEADS_B, T, T), BF16),
        ],
        compiler_params=_cparams(("parallel", "arbitrary")),
        name="stickbreak",
    )(qb, kb, vbt, g)


def _swa_kernel(sink_ref, qc_ref, kc_ref, vct_ref, g_ref, o_ref, s_scr, *, tile):
    T = tile
    NKEY = T + WINDOW
    i = pl.program_id(1)
    start = pl.multiple_of(jnp.maximum(i * T - WINDOW, 0), WINDOW)
    kidx = start + lax.broadcasted_iota(I32, (NKEY, T), 0)
    tq = i * T + lax.broadcasted_iota(I32, (NKEY, T), 1)
    d = tq - kidx
    band = (d >= 0) & (d < WINDOW)
    kc = kc_ref[0, pl.ds(start, NKEY), :]
    c0 = start // WINDOW
    per_kv = N_HEADS_C // N_KV_C
    for h in range(N_HEADS_C):
        s_scr[h] = jnp.dot(kc, qc_ref[0, 0, h // per_kv, :, (h % per_kv) * T:(h % per_kv + 1) * T],
                           preferred_element_type=F32)
    neg = jnp.where(band, 0.0, NEG_BIG).astype(BF16)
    ones = jnp.ones((16, WINDOW), BF16)
    outs = []
    for h in range(N_HEADS_C):
        g = h // per_kv
        sink = sink_ref[h] * LOG2E
        sb = s_scr[h].astype(BF16) + neg
        m = jnp.maximum(jnp.max(_fold16(sb, jnp.max), axis=0, keepdims=True).astype(F32), sink)
        p = jnp.exp2((s_scr[h] - m).astype(BF16) + neg)
        o = None
        for c in range(NKEY // WINDOW):
            vt1 = jnp.concatenate([vct_ref[0, c0 + c, 64 * g:64 * g + 64, :], ones], axis=0)
            part = jnp.dot(vt1, p[c * WINDOW:(c + 1) * WINDOW, :], preferred_element_type=F32)
            o = part if o is None else o + part
        den = o[HEAD_DIM:HEAD_DIM + 1] + jnp.exp2(sink - m)
        outs.append(o[0:HEAD_DIM] / den)
    _store_heads(o_ref, g_ref, outs)


def _swa(sinks, qc, kc, vct, g):
    b, nq = qc.shape[:2]
    T = ATT_TILE
    l = nq * T
    return pl.pallas_call(
        functools.partial(_swa_kernel, tile=T),
        grid=(b, nq),
        in_specs=[
            pl.BlockSpec(memory_space=pltpu.SMEM),
            pl.BlockSpec((1, 1, 2, 128, 4 * T), lambda bb, i: (bb, i, 0, 0, 0)),
            pl.BlockSpec((1, l, 128), lambda bb, i: (bb, 0, 0)),
            pl.BlockSpec((1, l // LANES, 128, LANES), lambda bb, i: (bb, 0, 0, 0)),
            pl.BlockSpec((1, 512), lambda bb, i: (0, 0)),
        ],
        out_specs=pl.BlockSpec((1, T, 512), lambda bb, i: (bb, i, 0)),
        out_shape=jax.ShapeDtypeStruct((b, l, 512), BF16),
        scratch_shapes=[pltpu.VMEM((N_HEADS_C, T + WINDOW, T), F32)],
        compiler_params=_cparams(("parallel", "arbitrary")),
        name="swa",
    )(sinks, qc, kc, vct, g)


def _ffn_chunks(d_ff):
    chunks, c0 = [], 0
    while c0 < d_ff:
        cw = min(1024, d_ff - c0)
        chunks.append((c0, cw))
        c0 += cw
    return chunks


def _merge_ffn_kernel(x_ref, oa_ref, ob_ref, oc_ref, mod_ref, ln_ref, woa_ref, wob_ref, woc_ref,
                      wg_ref, wu_ref, wd_ref, out_ref):
    x = x_ref[0]
    y = (jnp.dot(oa_ref[0], woa_ref[...], preferred_element_type=F32)
         + jnp.dot(ob_ref[0], wob_ref[...], preferred_element_type=F32)
         + jnp.dot(oc_ref[0], woc_ref[...], preferred_element_type=F32))
    x1 = x + mod_ref[0, 2:3, :] * y
    ms = jnp.mean(x1 * x1, axis=-1, keepdims=True)
    h = x1 * lax.rsqrt(ms + NORM_EPS) * ln_ref[...]
    hb = (h * (1.0 + mod_ref[0, 4:5, :]) + mod_ref[0, 3:4, :]).astype(BF16)
    acc = None
    for c0, cw in _ffn_chunks(wg_ref.shape[1]):
        gt = jnp.dot(hb, wg_ref[:, c0:c0 + cw], preferred_element_type=F32)
        up = jnp.dot(hb, wu_ref[:, c0:c0 + cw], preferred_element_type=F32)
        act = (gt * jax.nn.sigmoid(gt) * up).astype(BF16)
        part = jnp.dot(act, wd_ref[c0:c0 + cw, :], preferred_element_type=F32)
        acc = part if acc is None else acc + part
    out_ref[0] = x1 + mod_ref[0, 5:6, :] * acc


def _merge_ffn(x, oa, ob, oc, mod_l, ln, woa, wob, woc, wg, wu, wd):
    b, l, d = x.shape
    tm = TOK_TILE
    dff = wg.shape[1]
    const2 = lambda bb, i: (0, 0)
    once = pl.Buffered(1)
    return pl.pallas_call(
        _merge_ffn_kernel,
        grid=(b, l // tm),
        in_specs=[
            pl.BlockSpec((1, tm, d), lambda bb, i: (bb, i, 0)),
            pl.BlockSpec((1, tm, 256), lambda bb, i: (bb, i, 0)),
            pl.BlockSpec((1, tm, 256), lambda bb, i: (bb, i, 0)),
            pl.BlockSpec((1, tm, 512), lambda bb, i: (bb, i, 0)),
            pl.BlockSpec((1, 6, d), lambda bb, i: (bb, 0, 0)),
            pl.BlockSpec((1, d), const2),
            pl.BlockSpec((256, d), const2, pipeline_mode=once),
            pl.BlockSpec((256, d), const2, pipeline_mode=once),
            pl.BlockSpec((512, d), const2, pipeline_mode=once),
            pl.BlockSpec((d, dff), const2, pipeline_mode=once),
            pl.BlockSpec((d, dff), const2, pipeline_mode=once),
            pl.BlockSpec((dff, d), const2, pipeline_mode=once),
        ],
        out_specs=pl.BlockSpec((1, tm, d), lambda bb, i: (bb, i, 0)),
        out_shape=jax.ShapeDtypeStruct((b, l, d), F32),
        compiler_params=_cparams(("parallel", "arbitrary")),
        name="merge_ffn",
    )(x, oa, ob, oc, mod_l, ln, woa, wob, woc, wg, wu, wd)


def _rope_tables(l):
    inv = 1.0 / (ROPE_THETA ** (jnp.arange(0, HEAD_DIM, 2, dtype=F32) / HEAD_DIM))
    ang = jnp.arange(l, dtype=F32)[:, None] * inv[None, :]
    cos, sin = jnp.cos(ang), jnp.sin(ang)
    cos_n = jnp.tile(cos, (1, 4))
    sin_n = jnp.concatenate([-sin, sin, -sin, sin], axis=1)
    return cos.T, sin.T, cos_n, sin_n


def _split_w_in(w_in):
    widths = (256, 64, 64, 256, 64, 4, 256, 256, 256, 512, 128, 128)
    offs = [0]
    for w in widths:
        offs.append(offs[-1] + w)
    qa, ka, va, qi, ki, wi, qb, kb, vb, qc, kc, vc = [w_in[:, :, offs[k]:offs[k + 1]]
                                                      for k in range(len(widths))]
    idx_scale = float((N_IDX_HEADS * HEAD_DIM) ** -0.5)
    att_scale = float(HEAD_DIM ** -0.5)
    pad = jnp.zeros(wi.shape[:2] + (R_TOTAL - R_W - N_IDX_HEADS,), w_in.dtype)
    wt = jnp.concatenate([qa, qi, qb * (att_scale * LOG2E), qc, va, vb, vc, wi * idx_scale, pad],
                         axis=2)
    wt = jnp.swapaxes(wt, 1, 2).astype(BF16)
    wn = jnp.concatenate([ka, ki, kb, kc], axis=2).astype(BF16)
    return wt, wn


@jax.jit
def kernel(x, c, ln1, ln2, w_mod, b_mod, w_in, qn_a, kn_a, qn_c, kn_c, sinks, g_out, w_o,
           w_gate, w_up, w_down):
    depth = w_in.shape[0]
    b, l, d = x.shape
    assert l % TOK_TILE == 0 and l >= ATT_TILE + WINDOW
    topk = min(TOPK_MAX, l // 4)
    att_scale = float(HEAD_DIM ** -0.5)

    tabs = _rope_tables(l)
    mod = _modulation(c, w_mod, b_mod)
    wt, wn = _split_w_in(w_in)
    w_o_b = w_o.astype(BF16)
    w_g_b, w_u_b, w_d_b = w_gate.astype(BF16), w_up.astype(BF16), w_down.astype(BF16)
    ones = jnp.ones((HEAD_DIM,), F32)

    for li in range(depth):
        gains = ((qn_a[li] * (att_scale * LOG2E)).reshape(HEAD_DIM, 1),
                 (qn_c[li] * (att_scale * LOG2E)).reshape(HEAD_DIM, 1),
                 jnp.concatenate([kn_a[li], ones]).reshape(1, LANES),
                 jnp.concatenate([kn_c[li], kn_c[li]]).reshape(1, LANES))
        (qa, qi, qb, qc, w, vat, vbt, vct, kaki, kb, kc) = _inproj(
            x, mod[li], ln1[li].reshape(1, d), wt[li], wn[li], tabs, gains)
        g = g_out[li].reshape(1, -1)
        oa = _dsa(qi, qa, w, kaki, vat, g[:, 0:256], topk)
        ob = _sb(qb, kb, vbt, g[:, 256:512])
        oc = _swa(sinks[li], qc, kc, vct, g[:, 512:1024])
        x = _merge_ffn(x, oa, ob, oc, mod[li], ln2[li].reshape(1, d),
                       w_o_b[li, 0:256], w_o_b[li, 256:512], w_o_b[li, 512:1024],
                       w_g_b[li], w_u_b[li], w_d_b[li])
    return x
```

```python
import functools

import jax
import jax.numpy as jnp
from jax import lax
from jax.experimental import pallas as pl
from jax.experimental.pallas import tpu as pltpu

F32 = jnp.float32
BF16 = jnp.bfloat16
I32 = jnp.int32

HEAD_DIM = 64
N_HEADS_A = 4
N_HEADS_B = 4
N_HEADS_C = 8
N_KV_C = 2
N_IDX_HEADS = 4
TOPK_MAX = 256
WINDOW = 128
ROPE_THETA = 10000.0
NORM_EPS = 1e-6
LANES = 128
ATT_TILE = 256
TOK_TILE = 512
VMEM_LIMIT = 48 * 1024 * 1024

NEG_BIG = -1e30
INT_MIN = -2 ** 31
INT_MAX = 2 ** 31 - 1
KEY_NEG_INF = -0x7F800000
IDX_BIG = 2 ** 30
LOG2E = 1.4426950408889634
FIRST_PROBES = 14
PROBES_PER_CHECK = 2
FLOAT_PROBES = 30

R_QA, R_QI, R_QB, R_QC = 0, 256, 512, 768
R_VA, R_VB, R_VC, R_W = 1280, 1344, 1600, 1728
R_TOTAL = 1744
C_KAKI, C_KB, C_KC = 0, 128, 384
C_TOTAL = 512


def _cparams(sem):
    return pltpu.CompilerParams(dimension_semantics=sem, vmem_limit_bytes=VMEM_LIMIT)


def _rows_per_step(b):
    return 2 if b % 2 == 0 else 1


def _mod_kernel(c_ref, w_ref, b_ref, o_ref):
    c = c_ref[...]
    ca = (c * jax.nn.sigmoid(c)).astype(BF16)
    y = jnp.dot(ca, w_ref[0].astype(BF16), preferred_element_type=F32)
    o_ref[0, 0] = y + b_ref[0, 0]


def _modulation(c, w_mod, b_mod):
    depth, d, _ = w_mod.shape
    b = c.shape[0]
    out = pl.pallas_call(
        _mod_kernel,
        grid=(depth, 6),
        in_specs=[
            pl.BlockSpec((b, d), lambda l, k: (0, 0)),
            pl.BlockSpec((1, d, d), lambda l, k: (l, 0, k)),
            pl.BlockSpec((1, 1, 1, d), lambda l, k: (l, k, 0, 0)),
        ],
        out_specs=pl.BlockSpec((1, 1, b, d), lambda l, k: (l, k, 0, 0)),
        out_shape=jax.ShapeDtypeStruct((depth, 6, b, d), F32),
        compiler_params=_cparams(("arbitrary", "arbitrary")),
        name="modulation",
    )(c, w_mod, b_mod.reshape(depth, 6, 1, d))
    return jnp.transpose(out, (0, 2, 1, 3))


def _inproj_kernel(x_ref, mod_ref, ln_ref, wt_ref, wn_ref, cost_ref, sint_ref, cosn_ref, sinn_ref,
                   gqa_ref, gqc_ref, gka_ref, gkc_ref,
                   qa_ref, qi_ref, qb_ref, qc_ref, w_ref, vat_ref, vbt_ref, vct_ref,
                   kaki_ref, kb_ref, kc_ref, *, tile):
    T = tile
    nc = x_ref.shape[1] // T
    gqa = gqa_ref[...]
    gqc = gqc_ref[...]
    lane = lax.broadcasted_iota(I32, (T, LANES), 1)
    left = lane < HEAD_DIM
    first = (lane & 32) == 0
    zero_half = jnp.zeros((HEAD_DIM, T), BF16)
    per_kv = N_HEADS_C // N_KV_C

    for c in range(nc):
        tok = slice(c * T, (c + 1) * T)
        x = x_ref[0, tok, :]
        ms = jnp.mean(x * x, axis=-1, keepdims=True)
        h = x * lax.rsqrt(ms + NORM_EPS) * ln_ref[...]
        h = h * (1.0 + mod_ref[0, 1:2, :]) + mod_ref[0, 0:1, :]
        hb = h.astype(BF16)
        pt = lax.dot_general(wt_ref[...], hb, (((1,), (1,)), ((), ())),
                             preferred_element_type=F32)
        pn = jnp.dot(hb, wn_ref[...], preferred_element_type=F32)

        cos_t = cost_ref[:, tok]
        sin_t = sint_ref[:, tok]

        def rope_t(y):
            y1, y2 = y[0:32], y[32:64]
            return jnp.concatenate([y1 * cos_t - y2 * sin_t, y2 * cos_t + y1 * sin_t], axis=0)

        def rms_t(y, g):
            m = jnp.mean(y * y, axis=0, keepdims=True)
            return y * lax.rsqrt(m + NORM_EPS) * g

        def put_padded(ref, lead, y, half, col):
            idx = (0, c) + lead
            ref[idx + (slice(64 * half, 64 * half + 64), slice(col * T, (col + 1) * T))] = \
                y.astype(BF16)
            ref[idx + (slice(64 * (1 - half), 64 * (1 - half) + 64),
                       slice(col * T, (col + 1) * T))] = zero_half

        for hh in range(N_HEADS_A):
            put_padded(qa_ref, (), rope_t(rms_t(pt[R_QA + 64 * hh:R_QA + 64 * hh + 64], gqa)), 0, hh)
        for hh in range(N_IDX_HEADS):
            put_padded(qi_ref, (), rope_t(pt[R_QI + 64 * hh:R_QI + 64 * hh + 64]), 1, hh)
        for hh in range(N_HEADS_B):
            put_padded(qb_ref, (), pt[R_QB + 64 * hh:R_QB + 64 * hh + 64], hh % 2, hh)
        for hh in range(N_HEADS_C):
            g = hh // per_kv
            put_padded(qc_ref, (g,), rope_t(rms_t(pt[R_QC + 64 * hh:R_QC + 64 * hh + 64], gqc)),
                       g, hh % per_kv)
        vat_ref[0, c] = pt[R_VA:R_VA + 64].astype(BF16)
        vbt_ref[0, c] = pt[R_VB:R_VB + 256].astype(BF16)
        for hh in range(N_IDX_HEADS):
            w_ref[0, c, :, hh * T:(hh + 1) * T] = pt[R_W + hh:R_W + hh + 1]
        for cc in range(T // LANES):
            vct_ref[0, c * (T // LANES) + cc] = \
                pt[R_VC:R_VC + 128, cc * LANES:(cc + 1) * LANES].astype(BF16)

        cos_n = cosn_ref[tok, :]
        sin_n = sinn_ref[tok, :]

        def rope_n(y):
            rot = jnp.where(first, pltpu.roll(y, 96, 1), pltpu.roll(y, 32, 1))
            return y * cos_n + rot * sin_n

        def head_rsqrt(y):
            sq = y * y
            s0 = jnp.sum(jnp.where(left, sq, 0.0), axis=-1, keepdims=True)
            s1 = jnp.sum(jnp.where(left, 0.0, sq), axis=-1, keepdims=True)
            return lax.rsqrt(jnp.where(left, s0, s1) * (1.0 / HEAD_DIM) + NORM_EPS)

        y = pn[:, C_KAKI:C_KAKI + 128]
        fac = jnp.where(left, head_rsqrt(y), 1.0) * gka_ref[...]
        kaki_ref[0, tok, :] = rope_n(y * fac).astype(BF16)
        kb_ref[0, tok, :] = pn[:, C_KB:C_KB + 256].astype(BF16)
        y = pn[:, C_KC:C_KC + 128]
        kc_ref[0, tok, :] = rope_n(y * head_rsqrt(y) * gkc_ref[...]).astype(BF16)


def _inproj(x, mod_l, ln, wt, wn, tabs, gains):
    b, l, d = x.shape
    T = ATT_TILE
    tm = TOK_TILE
    nq = l // T
    nc = tm // T
    cos_t, sin_t, cos_n, sin_n = tabs
    gqa, gqc, gka, gkc = gains
    const2 = lambda bb, i: (0, 0)
    out_shape = (
        jax.ShapeDtypeStruct((b, nq, 128, 4 * T), BF16),
        jax.ShapeDtypeStruct((b, nq, 128, 4 * T), BF16),
        jax.ShapeDtypeStruct((b, nq, 128, 4 * T), BF16),
        jax.ShapeDtypeStruct((b, nq, 2, 128, 4 * T), BF16),
        jax.ShapeDtypeStruct((b, nq, 1, 4 * T), F32),
        jax.ShapeDtypeStruct((b, nq, 64, T), BF16),
        jax.ShapeDtypeStruct((b, nq, 256, T), BF16),
        jax.ShapeDtypeStruct((b, l // LANES, 128, LANES), BF16),
        jax.ShapeDtypeStruct((b, l, 128), BF16),
        jax.ShapeDtypeStruct((b, l, 256), BF16),
        jax.ShapeDtypeStruct((b, l, 128), BF16),
    )
    out_specs = (
        pl.BlockSpec((1, nc, 128, 4 * T), lambda bb, i: (bb, i, 0, 0)),
        pl.BlockSpec((1, nc, 128, 4 * T), lambda bb, i: (bb, i, 0, 0)),
        pl.BlockSpec((1, nc, 128, 4 * T), lambda bb, i: (bb, i, 0, 0)),
        pl.BlockSpec((1, nc, 2, 128, 4 * T), lambda bb, i: (bb, i, 0, 0, 0)),
        pl.BlockSpec((1, nc, 1, 4 * T), lambda bb, i: (bb, i, 0, 0)),
        pl.BlockSpec((1, nc, 64, T), lambda bb, i: (bb, i, 0, 0)),
        pl.BlockSpec((1, nc, 256, T), lambda bb, i: (bb, i, 0, 0)),
        pl.BlockSpec((1, tm // LANES, 128, LANES), lambda bb, i: (bb, i, 0, 0)),
        pl.BlockSpec((1, tm, 128), lambda bb, i: (bb, i, 0)),
        pl.BlockSpec((1, tm, 256), lambda bb, i: (bb, i, 0)),
        pl.BlockSpec((1, tm, 128), lambda bb, i: (bb, i, 0)),
    )
    in_specs = [
        pl.BlockSpec((1, tm, d), lambda bb, i: (bb, i, 0)),
        pl.BlockSpec((1, 6, d), lambda bb, i: (bb, 0, 0)),
        pl.BlockSpec((1, d), const2),
        pl.BlockSpec((R_TOTAL, d), const2),
        pl.BlockSpec((d, C_TOTAL), const2),
        pl.BlockSpec((32, tm), lambda bb, i: (0, i)),
        pl.BlockSpec((32, tm), lambda bb, i: (0, i)),
        pl.BlockSpec((tm, LANES), lambda bb, i: (i, 0)),
        pl.BlockSpec((tm, LANES), lambda bb, i: (i, 0)),
        pl.BlockSpec((HEAD_DIM, 1), const2),
        pl.BlockSpec((HEAD_DIM, 1), const2),
        pl.BlockSpec((1, LANES), const2),
        pl.BlockSpec((1, LANES), const2),
    ]
    return pl.pallas_call(
        functools.partial(_inproj_kernel, tile=T),
        grid=(b, l // tm),
        in_specs=in_specs,
        out_specs=out_specs,
        out_shape=out_shape,
        compiler_params=_cparams(("parallel", "arbitrary")),
        name="inproj",
    )(x, mod_l, ln, wt, wn, cos_t, sin_t, cos_n, sin_n, gqa, gqc, gka, gkc)


def _store_heads(o_ref, g_ref, heads_t, row):
    for p in range(len(heads_t) // 2):
        parts = []
        for o in heads_t[2 * p:2 * p + 2]:
            ms = jnp.mean(o * o, axis=0, keepdims=True)
            parts.append(o * lax.rsqrt(ms + NORM_EPS))
        ot = jnp.concatenate(parts, axis=0).T
        o_ref[row, :, 128 * p:128 * p + 128] = (ot * g_ref[:, 128 * p:128 * p + 128]).astype(BF16)


def _float_key(x):
    kb = pltpu.bitcast(x, I32)
    return jnp.where(kb < 0, INT_MIN - kb, kb)


def _key_float(k):
    return pltpu.bitcast(jnp.where(k < 0, INT_MIN - k, k), F32)


def _fold16(x, op):
    t = x.shape[0]
    return op(x.reshape(t // 16, 16, x.shape[1]), axis=0)


def _fold8(x, op):
    t = x.shape[0]
    return op(x.reshape(t // 8, 8, x.shape[1]), axis=0)


def _dsa_kernel(qi_ref, qa_ref, w_ref, kaki_ref, vat_ref, g_ref, o_ref,
                key_scr, s_scr, r_scr, p_scr, ib_scr, m_scr, l_scr, acc_scr, *, tile, topk, nb):
    T = tile
    K = topk
    NB = range(nb)
    i = pl.program_id(1)
    nk = i + 1
    rowk = lax.broadcasted_iota(I32, (T, T), 0)
    colq = lax.broadcasted_iota(I32, (T, T), 1)
    negk = -rowk

    def key_chunk(b, j):
        return kaki_ref[b, pl.ds(pl.multiple_of(j * T, T), T), :]

    def index_dots(j):
        for b in NB:
            kc = key_chunk(b, j)
            for h in range(N_IDX_HEADS):
                s_scr[b, h] = jnp.dot(kc, qi_ref[b, 0, :, h * T:(h + 1) * T],
                                      preferred_element_type=F32)

    def attn_dots(j):
        for b in NB:
            kc = key_chunk(b, j)
            for h in range(N_HEADS_A):
                r_scr[b, h] = jnp.dot(kc, qa_ref[b, 0, :, h * T:(h + 1) * T],
                                      preferred_element_type=F32)

    def score_chunk(j, nxt, diag, mm):
        scs = []
        for b in NB:
            sc = None
            for h in range(N_IDX_HEADS):
                r = jnp.maximum(s_scr[b, h], 0.0) * w_ref[b, 0, :, h * T:(h + 1) * T]
                sc = r if sc is None else sc + r
            scs.append(sc)
        index_dots(nxt)
        out = []
        for b in NB:
            sc = scs[b]
            key = jnp.where(sc == 0.0, negk - j * T, _float_key(sc))
            if diag:
                causal = rowk <= colq
                key_hi = jnp.where(causal, key, KEY_NEG_INF)
                key_lo = jnp.where(causal, key, INT_MAX)
            else:
                key_hi = key_lo = key
            key_scr[b, j] = key_hi
            mx8, mn8 = mm[b]
            out.append((jnp.maximum(mx8, _fold8(key_hi, jnp.max)),
                        jnp.minimum(mn8, _fold8(key_lo, jnp.min))))
        return tuple(out)

    index_dots(i)
    attn_dots(0)
    mm0 = tuple((jnp.full((8, T), INT_MIN, I32), jnp.full((8, T), INT_MAX, I32)) for _ in NB)
    mm = score_chunk(i, 0, True, mm0)
    mm = lax.fori_loop(0, i, lambda j, c: score_chunk(j, jnp.minimum(j + 1, i - 1), False, c), mm)

    def count_ge(mids):
        def body(p, accs):
            out = []
            for b in NB:
                a = _fold8(jnp.where(key_scr[b, 2 * p] >= mids[b], 1, 0), jnp.sum)
                c = _fold8(jnp.where(key_scr[b, 2 * p + 1] >= mids[b], 1, 0), jnp.sum)
                out.append(accs[b] + (a + c))
            return tuple(out)
        accs = lax.fori_loop(0, (nk + 1) // 2, body, tuple(jnp.zeros((8, T), I32) for _ in NB))
        return [jnp.sum(a, axis=0, keepdims=True) for a in accs]

    @pl.when(nk % 2 == 1)
    def _():
        for b in NB:
            key_scr[b, nk] = jnp.full((T, T), INT_MIN, I32)

    tq = i * T + lax.broadcasted_iota(I32, (1, T), 1)
    big = tq >= K
    zero_lo = -nk * T

    def pick(it, lo, hi, clo, chi):
        f = ((clo - K).astype(F32) + 0.5) / jnp.maximum(clo - chi, 1).astype(F32)
        frac = jnp.where(f < 0.25, jnp.maximum(2.0 * f, 1e-3),
                         jnp.where(f > 0.75, 1.0 - jnp.maximum(2.0 - 2.0 * f, 1e-3), 0.5))
        fmid = _float_key(_key_float(lo) * (1.0 - frac) + _key_float(hi) * frac)
        kmid = (lo >> 1) + (hi >> 1) + (lo & hi & 1)
        inside = jnp.where(fmid > lo, jnp.where(fmid < hi, it, FLOAT_PROBES), FLOAT_PROBES)
        mid = jnp.where(inside < FLOAT_PROBES, fmid, kmid)
        mid = jnp.where(lo == 0, jnp.where(hi > 1, 1, mid), mid)
        mid = jnp.where(lo < zero_lo, jnp.where(hi > zero_lo, zero_lo, mid), mid)
        return jnp.where(lo < 0, jnp.where(hi > 0, 0, mid), mid)

    def probes(n, st):
        _, it, brs = st
        for _ in range(n):
            mids = [pick(it, *brs[b]) for b in NB]
            cs = count_ge(mids)
            new = []
            for b in NB:
                lo, hi, clo, chi = brs[b]
                ge = cs[b] >= K
                new.append((jnp.where(ge, mids[b], lo), jnp.where(ge, hi, mids[b]),
                            jnp.where(ge, cs[b], clo), jnp.where(ge, chi, cs[b])))
            brs = tuple(new)
            it = it + 1
        nopen = 0.0
        for b in NB:
            lo, hi, clo, chi = brs[b]
            nopen = nopen + jnp.sum(jnp.where(clo == K, 0.0, jnp.where((hi - 1) == lo, 0.0, 1.0)))
        return nopen, it, brs

    brs0 = tuple((jnp.where(big, jnp.min(mm[b][1], axis=0, keepdims=True), KEY_NEG_INF),
                  jnp.where(big, jnp.max(mm[b][0], axis=0, keepdims=True) + 1, KEY_NEG_INF + 1),
                  tq + 1, jnp.zeros((1, T), I32)) for b in NB)
    st = probes(FIRST_PROBES, (0.0, jnp.zeros((1, T), I32), brs0))
    _, _, brs = lax.while_loop(lambda st: st[0] > 0.0,
                               functools.partial(probes, PROBES_PER_CHECK), st)

    ties = [jnp.where(big, jnp.where(brs[b][2] > K, 1.0, 0.0), 0.0) for b in NB]
    thrs = [jnp.maximum(brs[b][0], KEY_NEG_INF + 1) for b in NB]
    ntie = 0.0
    for b in NB:
        ib_scr[b] = jnp.full((1, T), IDX_BIG, I32)
        ntie = ntie + jnp.sum(ties[b])

    @pl.when(ntie > 0.0)
    def _():
        before = jnp.where(colq < rowk, 1.0, 0.0).astype(BF16)
        for b in NB:
            need = (K - brs[b][3]).astype(F32)

            def body(j, carry, b=b, need=need):
                run, ibm8 = carry
                eq = jnp.where(key_scr[b, j] == thrs[b], 1.0, 0.0)
                pc = jnp.dot(before, eq.astype(BF16), preferred_element_type=F32) + run
                idx1 = (rowk + (j * T + 1)).astype(F32)
                taken = jnp.where(pc < need, eq * idx1, 0.0)
                return (run + jnp.sum(_fold8(eq, jnp.sum), axis=0, keepdims=True),
                        jnp.maximum(ibm8, _fold8(taken, jnp.max)))

            _, ibm8 = lax.fori_loop(0, nk, body,
                                    (jnp.zeros((1, T), F32), jnp.zeros((8, T), F32)))
            ibm = jnp.max(ibm8, axis=0, keepdims=True).astype(I32)
            ib_scr[b] = jnp.where(ties[b] > 0.0, ibm, IDX_BIG)

    ibs = [ib_scr[b] for b in NB]

    m_scr[...] = jnp.full(m_scr.shape, NEG_BIG, F32)
    l_scr[...] = jnp.zeros(l_scr.shape, F32)
    acc_scr[...] = jnp.zeros(acc_scr.shape, F32)

    def attn_body(j, carry):
        m_new = {}
        for b in NB:
            t_el = jnp.where((rowk + j * T) < ibs[b], thrs[b], thrs[b] + 1)
            neg = jnp.where(key_scr[b, j] >= t_el, 0.0, NEG_BIG).astype(BF16)
            for h in range(N_HEADS_A):
                sb = r_scr[b, h].astype(BF16) + neg
                mc = jnp.max(_fold16(sb, jnp.max), axis=0, keepdims=True).astype(F32)
                m_new[b, h] = jnp.maximum(m_scr[b, h], mc)
            for h in range(N_HEADS_A):
                p_scr[b, h] = jnp.exp2((r_scr[b, h] - m_new[b, h]).astype(BF16) + neg)
        attn_dots(jnp.minimum(j + 1, i))
        for b in NB:
            vt1 = jnp.concatenate([vat_ref[b, j], jnp.ones((16, T), BF16)], axis=0)
            for h in range(N_HEADS_A):
                alpha = jnp.exp2(m_scr[b, h] - m_new[b, h])
                pv = jnp.dot(vt1, p_scr[b, h], preferred_element_type=F32)
                l_scr[b, h] = alpha * l_scr[b, h] + pv[HEAD_DIM:HEAD_DIM + 1]
                acc_scr[b, h] = alpha * acc_scr[b, h] + pv[0:HEAD_DIM]
                m_scr[b, h] = m_new[b, h]
        return carry

    lax.fori_loop(0, nk, attn_body, 0)
    for b in NB:
        _store_heads(o_ref, g_ref, [acc_scr[b, h] / l_scr[b, h] for h in range(N_HEADS_A)], b)


def _dsa(qi, qa, w, kaki, vat, g, topk):
    b, nq, _, _ = qa.shape
    T = ATT_TILE
    l = nq * T
    nb = _rows_per_step(b)
    return pl.pallas_call(
        functools.partial(_dsa_kernel, tile=T, topk=topk, nb=nb),
        grid=(b // nb, nq),
        in_specs=[
            pl.BlockSpec((nb, 1, 128, 4 * T), lambda bb, i: (bb, i, 0, 0)),
            pl.BlockSpec((nb, 1, 128, 4 * T), lambda bb, i: (bb, i, 0, 0)),
            pl.BlockSpec((nb, 1, 1, 4 * T), lambda bb, i: (bb, i, 0, 0)),
            pl.BlockSpec((nb, l, 128), lambda bb, i: (bb, 0, 0)),
            pl.BlockSpec((nb, nq, 64, T), lambda bb, i: (bb, 0, 0, 0)),
            pl.BlockSpec((1, 256), lambda bb, i: (0, 0)),
        ],
        out_specs=pl.BlockSpec((nb, T, 256), lambda bb, i: (bb, i, 0)),
        out_shape=jax.ShapeDtypeStruct((b, l, 256), BF16),
        scratch_shapes=[
            pltpu.VMEM((nb, nq + nq % 2, T, T), I32),
            pltpu.VMEM((nb, N_HEADS_A, T, T), F32),
            pltpu.VMEM((nb, N_HEADS_A, T, T), F32),
            pltpu.VMEM((nb, N_HEADS_A, T, T), BF16),
            pltpu.VMEM((nb, 1, T), I32),
            pltpu.VMEM((nb, N_HEADS_A, 1, T), F32),
            pltpu.VMEM((nb, N_HEADS_A, 1, T), F32),
            pltpu.VMEM((nb, N_HEADS_A, HEAD_DIM, T), F32),
        ],
        compiler_params=_cparams(("parallel", "arbitrary")),
        name="dsa",
    )(qi, qa, w, kaki, vat, g)


def _sb_kernel(qb_ref, kb_ref, vbt_ref, g_ref, o_ref, run_scr, acc_scr, z_scr, e_scr, c_scr, hl_scr,
               *, tile, nb):
    T = tile
    NB = range(nb)
    i = pl.program_id(1)
    rowk = lax.broadcasted_iota(I32, (T, T), 0)
    colq = lax.broadcasted_iota(I32, (T, T), 1)
    strict = rowk < colq
    tmat = jnp.where(colq > rowk, 1.0, 0.0).astype(BF16)
    run_scr[...] = jnp.zeros(run_scr.shape, F32)
    acc_scr[...] = jnp.zeros(acc_scr.shape, F32)

    def scores(j):
        row0 = pl.multiple_of(j * T, T)
        for b in NB:
            for h in range(N_HEADS_B):
                p = h // 2
                kc = kb_ref[b, pl.ds(row0, T), 128 * p:128 * p + 128]
                z_scr[b, h] = jnp.dot(kc, qb_ref[b, 0, :, h * T:(h + 1) * T],
                                      preferred_element_type=F32)

    def chunk(j, diag):
        tots = {}
        for b in NB:
            for h in range(N_HEADS_B):
                z = z_scr[b, h]
                nz = -z
                lk = jnp.minimum(nz, 0.0) - jnp.log2(1.0 + jnp.exp2(jnp.minimum(z, nz)))
                if diag:
                    lk = jnp.where(strict, lk, 0.0)
                hl_scr[b, h] = lk.astype(BF16)
                e_scr[b, h] = z + lk
                c_scr[b, h] = jnp.dot(tmat, hl_scr[b, h], preferred_element_type=F32)
                tots[b, h] = c_scr[b, h][0:1, :] + lk[0:1, :]
        scores(jnp.maximum(j - 1, 0))
        for b in NB:
            for h in range(N_HEADS_B):
                a = jnp.exp2(e_scr[b, h] + c_scr[b, h] + run_scr[b, h])
                if diag:
                    a = jnp.where(strict, a, 0.0)
                acc_scr[b, h] = acc_scr[b, h] + jnp.dot(vbt_ref[b, j, 64 * h:64 * h + 64, :],
                                                        a.astype(BF16), preferred_element_type=F32)
                run_scr[b, h] = run_scr[b, h] + tots[b, h]

    scores(i)
    chunk(i, True)

    def body(jj, carry):
        chunk(i - 1 - jj, False)
        return carry

    lax.fori_loop(0, i, body, 0)
    for b in NB:
        _store_heads(o_ref, g_ref, [acc_scr[b, h] for h in range(N_HEADS_B)], b)


def _sb(qb, kb, vbt, g):
    b, nq, _, _ = qb.shape
    T = ATT_TILE
    l = nq * T
    nb = _rows_per_step(b)
    return pl.pallas_call(
        functools.partial(_sb_kernel, tile=T, nb=nb),
        grid=(b // nb, nq),
        in_specs=[
            pl.BlockSpec((nb, 1, 128, 4 * T), lambda bb, i: (bb, i, 0, 0)),
            pl.BlockSpec((nb, l, 256), lambda bb, i: (bb, 0, 0)),
            pl.BlockSpec((nb, nq, 256, T), lambda bb, i: (bb, 0, 0, 0)),
            pl.BlockSpec((1, 256), lambda bb, i: (0, 0)),
        ],
        out_specs=pl.BlockSpec((nb, T, 256), lambda bb, i: (bb, i, 0)),
        out_shape=jax.ShapeDtypeStruct((b, l, 256), BF16),
        scratch_shapes=[
            pltpu.VMEM((nb, N_HEADS_B, 1, T), F32),
            pltpu.VMEM((nb, N_HEADS_B, HEAD_DIM, T), F32),
            pltpu.VMEM((nb, N_HEADS_B, T, T), F32),
            pltpu.VMEM((nb, N_HEADS_B, T, T), F32),
            pltpu.VMEM((nb, N_HEADS_B, T, T), F32),
            pltpu.VMEM((nb, N_HEADS_B, T, T), BF16),
        ],
        compiler_params=_cparams(("parallel", "arbitrary")),
        name="stickbreak",
    )(qb, kb, vbt, g)


def _swa_kernel(sink_ref, qc_ref, kc_ref, vct_ref, g_ref, o_ref, s_scr, *, tile, nb):
    T = tile
    NKEY = T + WINDOW
    NB = range(nb)
    i = pl.program_id(1)
    start = pl.multiple_of(jnp.maximum(i * T - WINDOW, 0), WINDOW)
    kidx = start + lax.broadcasted_iota(I32, (NKEY, T), 0)
    tq = i * T + lax.broadcasted_iota(I32, (NKEY, T), 1)
    d = tq - kidx
    band = (d >= 0) & (d < WINDOW)
    c0 = start // WINDOW
    per_kv = N_HEADS_C // N_KV_C
    for b in NB:
        kc = kc_ref[b, pl.ds(start, NKEY), :]
        for h in range(N_HEADS_C):
            s_scr[b, h] = jnp.dot(
                kc, qc_ref[b, 0, h // per_kv, :, (h % per_kv) * T:(h % per_kv + 1) * T],
                preferred_element_type=F32)
    neg = jnp.where(band, 0.0, NEG_BIG).astype(BF16)
    ones = jnp.ones((16, WINDOW), BF16)
    for b in NB:
        outs = []
        for h in range(N_HEADS_C):
            g = h // per_kv
            sink = sink_ref[h] * LOG2E
            sb = s_scr[b, h].astype(BF16) + neg
            m = jnp.maximum(jnp.max(_fold16(sb, jnp.max), axis=0, keepdims=True).astype(F32), sink)
            p = jnp.exp2((s_scr[b, h] - m).astype(BF16) + neg)
            o = None
            for c in range(NKEY // WINDOW):
                vt1 = jnp.concatenate([vct_ref[b, c0 + c, 64 * g:64 * g + 64, :], ones], axis=0)
                part = jnp.dot(vt1, p[c * WINDOW:(c + 1) * WINDOW, :],
                               preferred_element_type=F32)
                o = part if o is None else o + part
            den = o[HEAD_DIM:HEAD_DIM + 1] + jnp.exp2(sink - m)
            outs.append(o[0:HEAD_DIM] / den)
        _store_heads(o_ref, g_ref, outs, b)


def _swa(sinks, qc, kc, vct, g):
    b, nq = qc.shape[:2]
    T = ATT_TILE
    l = nq * T
    nb = _rows_per_step(b)
    return pl.pallas_call(
        functools.partial(_swa_kernel, tile=T, nb=nb),
        grid=(b // nb, nq),
        in_specs=[
            pl.BlockSpec(memory_space=pltpu.SMEM),
            pl.BlockSpec((nb, 1, 2, 128, 4 * T), lambda bb, i: (bb, i, 0, 0, 0)),
            pl.BlockSpec((nb, l, 128), lambda bb, i: (bb, 0, 0)),
            pl.BlockSpec((nb, l // LANES, 128, LANES), lambda bb, i: (bb, 0, 0, 0)),
            pl.BlockSpec((1, 512), lambda bb, i: (0, 0)),
        ],
        out_specs=pl.BlockSpec((nb, T, 512), lambda bb, i: (bb, i, 0)),
        out_shape=jax.ShapeDtypeStruct((b, l, 512), BF16),
        scratch_shapes=[pltpu.VMEM((nb, N_HEADS_C, T + WINDOW, T), F32)],
        compiler_params=_cparams(("parallel", "arbitrary")),
        name="swa",
    )(sinks, qc, kc, vct, g)


def _ffn_chunks(d_ff):
    chunks, c0 = [], 0
    while c0 < d_ff:
        cw = min(1024, d_ff - c0)
        chunks.append((c0, cw))
        c0 += cw
    return chunks


def _merge_ffn_kernel(x_ref, oa_ref, ob_ref, oc_ref, mod_ref, ln_ref, woa_ref, wob_ref, woc_ref,
                      wg_ref, wu_ref, wd_ref, out_ref):
    x = x_ref[0]
    y = (jnp.dot(oa_ref[0], woa_ref[...], preferred_element_type=F32)
         + jnp.dot(ob_ref[0], wob_ref[...], preferred_element_type=F32)
         + jnp.dot(oc_ref[0], woc_ref[...], preferred_element_type=F32))
    x1 = x + mod_ref[0, 2:3, :] * y
    ms = jnp.mean(x1 * x1, axis=-1, keepdims=True)
    h = x1 * lax.rsqrt(ms + NORM_EPS) * ln_ref[...]
    hb = (h * (1.0 + mod_ref[0, 4:5, :]) + mod_ref[0, 3:4, :]).astype(BF16)
    acc = None
    for c0, cw in _ffn_chunks(wg_ref.shape[1]):
        gt = jnp.dot(hb, wg_ref[:, c0:c0 + cw], preferred_element_type=F32)
        up = jnp.dot(hb, wu_ref[:, c0:c0 + cw], preferred_element_type=F32)
        act = (gt * jax.nn.sigmoid(gt) * up).astype(BF16)
        part = jnp.dot(act, wd_ref[c0:c0 + cw, :], preferred_element_type=F32)
        acc = part if acc is None else acc + part
    out_ref[0] = x1 + mod_ref[0, 5:6, :] * acc


def _merge_ffn(x, oa, ob, oc, mod_l, ln, woa, wob, woc, wg, wu, wd):
    b, l, d = x.shape
    tm = TOK_TILE
    dff = wg.shape[1]
    const2 = lambda bb, i: (0, 0)
    once = pl.Buffered(1)
    return pl.pallas_call(
        _merge_ffn_kernel,
        grid=(b, l // tm),
        in_specs=[
            pl.BlockSpec((1, tm, d), lambda bb, i: (bb, i, 0)),
            pl.BlockSpec((1, tm, 256), lambda bb, i: (bb, i, 0)),
            pl.BlockSpec((1, tm, 256), lambda bb, i: (bb, i, 0)),
            pl.BlockSpec((1, tm, 512), lambda bb, i: (bb, i, 0)),
            pl.BlockSpec((1, 6, d), lambda bb, i: (bb, 0, 0)),
            pl.BlockSpec((1, d), const2),
            pl.BlockSpec((256, d), const2, pipeline_mode=once),
            pl.BlockSpec((256, d), const2, pipeline_mode=once),
            pl.BlockSpec((512, d), const2, pipeline_mode=once),
            pl.BlockSpec((d, dff), const2, pipeline_mode=once),
            pl.BlockSpec((d, dff), const2, pipeline_mode=once),
            pl.BlockSpec((dff, d), const2, pipeline_mode=once),
        ],
        out_specs=pl.BlockSpec((1, tm, d), lambda bb, i: (bb, i, 0)),
        out_shape=jax.ShapeDtypeStruct((b, l, d), F32),
        compiler_params=_cparams(("parallel", "arbitrary")),
        name="merge_ffn",
    )(x, oa, ob, oc, mod_l, ln, woa, wob, woc, wg, wu, wd)


def _rope_tables(l):
    inv = 1.0 / (ROPE_THETA ** (jnp.arange(0, HEAD_DIM, 2, dtype=F32) / HEAD_DIM))
    ang = jnp.arange(l, dtype=F32)[:, None] * inv[None, :]
    cos, sin = jnp.cos(ang), jnp.sin(ang)
    cos_n = jnp.tile(cos, (1, 4))
    sin_n = jnp.concatenate([-sin, sin, -sin, sin], axis=1)
    return cos.T, sin.T, cos_n, sin_n


def _split_w_in(w_in):
    widths = (256, 64, 64, 256, 64, 4, 256, 256, 256, 512, 128, 128)
    offs = [0]
    for w in widths:
        offs.append(offs[-1] + w)
    qa, ka, va, qi, ki, wi, qb, kb, vb, qc, kc, vc = [w_in[:, :, offs[k]:offs[k + 1]]
                                                      for k in range(len(widths))]
    idx_scale = float((N_IDX_HEADS * HEAD_DIM) ** -0.5)
    att_scale = float(HEAD_DIM ** -0.5)
    pad = jnp.zeros(wi.shape[:2] + (R_TOTAL - R_W - N_IDX_HEADS,), w_in.dtype)
    wt = jnp.concatenate([qa, qi, qb * (att_scale * LOG2E), qc, va, vb, vc, wi * idx_scale, pad],
                         axis=2)
    wt = jnp.swapaxes(wt, 1, 2).astype(BF16)
    wn = jnp.concatenate([ka, ki, kb, kc], axis=2).astype(BF16)
    return wt, wn


@jax.jit
def kernel(x, c, ln1, ln2, w_mod, b_mod, w_in, qn_a, kn_a, qn_c, kn_c, sinks, g_out, w_o,
           w_gate, w_up, w_down):
    depth = w_in.shape[0]
    b, l, d = x.shape
    assert l % TOK_TILE == 0 and l >= ATT_TILE + WINDOW
    topk = min(TOPK_MAX, l // 4)
    att_scale = float(HEAD_DIM ** -0.5)

    tabs = _rope_tables(l)
    mod = _modulation(c, w_mod, b_mod)
    wt, wn = _split_w_in(w_in)
    w_o_b = w_o.astype(BF16)
    w_g_b, w_u_b, w_d_b = w_gate.astype(BF16), w_up.astype(BF16), w_down.astype(BF16)
    ones = jnp.ones((HEAD_DIM,), F32)

    for li in range(depth):
        gains = ((qn_a[li] * (att_scale * LOG2E)).reshape(HEAD_DIM, 1),
                 (qn_c[li] * (att_scale * LOG2E)).reshape(HEAD_DIM, 1),
                 jnp.concatenate([kn_a[li], ones]).reshape(1, LANES),
                 jnp.concatenate([kn_c[li], kn_c[li]]).reshape(1, LANES))
        (qa, qi, qb, qc, w, vat, vbt, vct, kaki, kb, kc) = _inproj(
            x, mod[li], ln1[li].reshape(1, d), wt[li], wn[li], tabs, gains)
        g = g_out[li].reshape(1, -1)
        oa = _dsa(qi, qa, w, kaki, vat, g[:, 0:256], topk)
        ob = _sb(qb, kb, vbt, g[:, 256:512])
        oc = _swa(sinks[li], qc, kc, vct, g[:, 512:1024])
        x = _merge_ffn(x, oa, ob, oc, mod[li], ln2[li].reshape(1, d),
                       w_o_b[li, 0:256], w_o_b[li, 256:512], w_o_b[li, 512:1024],
                       w_g_b[li], w_u_b[li], w_d_b[li])
    return x
```

```python
import functools

import jax
import jax.numpy as jnp
from jax import lax
from jax.experimental import pallas as pl
from jax.experimental.pallas import tpu as pltpu

F32 = jnp.float32
BF16 = jnp.bfloat16
I32 = jnp.int32

HEAD_DIM = 64
N_HEADS_A = 4
N_HEADS_B = 4
N_HEADS_C = 8
N_KV_C = 2
N_IDX_HEADS = 4
TOPK_MAX = 256
WINDOW = 128
ROPE_THETA = 10000.0
NORM_EPS = 1e-6
LANES = 128
ATT_TILE = 256
TOK_TILE = 512
VMEM_LIMIT = 48 * 1024 * 1024

NEG_BIG = -1e30
INT_MIN = -2 ** 31
INT_MAX = 2 ** 31 - 1
KEY_NEG_INF = -0x7F800000
IDX_BIG = 2 ** 30
LOG2E = 1.4426950408889634
FIRST_PROBES = 14
PROBES_PER_CHECK = 2
FLOAT_PROBES = 30

R_QA, R_QI, R_QB, R_QC = 0, 256, 512, 768
R_VA, R_VB, R_VC, R_W = 1280, 1344, 1600, 1728
R_TOTAL = 1744
C_KAKI, C_KB, C_KC = 0, 128, 384
C_TOTAL = 512


def _cparams(sem):
    return pltpu.CompilerParams(dimension_semantics=sem, vmem_limit_bytes=VMEM_LIMIT)


def _rows_per_step(b):
    return 4 if b % 4 == 0 else (2 if b % 2 == 0 else 1)


def _mod_kernel(c_ref, w_ref, b_ref, o_ref):
    c = c_ref[...]
    ca = (c * jax.nn.sigmoid(c)).astype(BF16)
    y = jnp.dot(ca, w_ref[0].astype(BF16), preferred_element_type=F32)
    o_ref[0, 0] = y + b_ref[0, 0]


def _modulation(c, w_mod, b_mod):
    depth, d, _ = w_mod.shape
    b = c.shape[0]
    out = pl.pallas_call(
        _mod_kernel,
        grid=(depth, 6),
        in_specs=[
            pl.BlockSpec((b, d), lambda l, k: (0, 0)),
            pl.BlockSpec((1, d, d), lambda l, k: (l, 0, k)),
            pl.BlockSpec((1, 1, 1, d), lambda l, k: (l, k, 0, 0)),
        ],
        out_specs=pl.BlockSpec((1, 1, b, d), lambda l, k: (l, k, 0, 0)),
        out_shape=jax.ShapeDtypeStruct((depth, 6, b, d), F32),
        compiler_params=_cparams(("arbitrary", "arbitrary")),
        name="modulation",
    )(c, w_mod, b_mod.reshape(depth, 6, 1, d))
    return jnp.transpose(out, (0, 2, 1, 3))


def _inproj_kernel(x_ref, mod_ref, ln_ref, wt_ref, wn_ref, cost_ref, sint_ref, cosn_ref, sinn_ref,
                   gqa_ref, gqc_ref, gka_ref, gkc_ref,
                   qa_ref, qi_ref, qb_ref, qc_ref, w_ref, vat_ref, vbt_ref, vct_ref,
                   kaki_ref, kb_ref, kc_ref, *, tile):
    T = tile
    nc = x_ref.shape[1] // T
    gqa = gqa_ref[...]
    gqc = gqc_ref[...]
    lane = lax.broadcasted_iota(I32, (T, LANES), 1)
    left = lane < HEAD_DIM
    first = (lane & 32) == 0
    zero_half = jnp.zeros((HEAD_DIM, T), BF16)
    per_kv = N_HEADS_C // N_KV_C

    for c in range(nc):
        tok = slice(c * T, (c + 1) * T)
        x = x_ref[0, tok, :]
        ms = jnp.mean(x * x, axis=-1, keepdims=True)
        h = x * lax.rsqrt(ms + NORM_EPS) * ln_ref[...]
        h = h * (1.0 + mod_ref[0, 1:2, :]) + mod_ref[0, 0:1, :]
        hb = h.astype(BF16)
        pt = lax.dot_general(wt_ref[...], hb, (((1,), (1,)), ((), ())),
                             preferred_element_type=F32)
        pn = jnp.dot(hb, wn_ref[...], preferred_element_type=F32)

        cos_t = cost_ref[:, tok]
        sin_t = sint_ref[:, tok]

        def rope_t(y):
            y1, y2 = y[0:32], y[32:64]
            return jnp.concatenate([y1 * cos_t - y2 * sin_t, y2 * cos_t + y1 * sin_t], axis=0)

        def rms_t(y, g):
            m = jnp.mean(y * y, axis=0, keepdims=True)
            return y * lax.rsqrt(m + NORM_EPS) * g

        def put_padded(ref, lead, y, half, col):
            idx = (0, c) + lead
            ref[idx + (slice(64 * half, 64 * half + 64), slice(col * T, (col + 1) * T))] = \
                y.astype(BF16)
            ref[idx + (slice(64 * (1 - half), 64 * (1 - half) + 64),
                       slice(col * T, (col + 1) * T))] = zero_half

        for hh in range(N_HEADS_A):
            put_padded(qa_ref, (), rope_t(rms_t(pt[R_QA + 64 * hh:R_QA + 64 * hh + 64], gqa)), 0, hh)
        for hh in range(N_IDX_HEADS):
            put_padded(qi_ref, (), rope_t(pt[R_QI + 64 * hh:R_QI + 64 * hh + 64]), 1, hh)
        for hh in range(N_HEADS_B):
            put_padded(qb_ref, (), pt[R_QB + 64 * hh:R_QB + 64 * hh + 64], hh % 2, hh)
        for hh in range(N_HEADS_C):
            g = hh // per_kv
            put_padded(qc_ref, (g,), rope_t(rms_t(pt[R_QC + 64 * hh:R_QC + 64 * hh + 64], gqc)),
                       g, hh % per_kv)
        vat_ref[0, c] = pt[R_VA:R_VA + 64].astype(BF16)
        vbt_ref[0, c] = pt[R_VB:R_VB + 256].astype(BF16)
        for hh in range(N_IDX_HEADS):
            w_ref[0, c, :, hh * T:(hh + 1) * T] = pt[R_W + hh:R_W + hh + 1]
        for cc in range(T // LANES):
            vct_ref[0, c * (T // LANES) + cc] = \
                pt[R_VC:R_VC + 128, cc * LANES:(cc + 1) * LANES].astype(BF16)

        cos_n = cosn_ref[tok, :]
        sin_n = sinn_ref[tok, :]

        def rope_n(y):
            rot = jnp.where(first, pltpu.roll(y, 96, 1), pltpu.roll(y, 32, 1))
            return y * cos_n + rot * sin_n

        def head_rsqrt(y):
            sq = y * y
            s0 = jnp.sum(jnp.where(left, sq, 0.0), axis=-1, keepdims=True)
            s1 = jnp.sum(jnp.where(left, 0.0, sq), axis=-1, keepdims=True)
            return lax.rsqrt(jnp.where(left, s0, s1) * (1.0 / HEAD_DIM) + NORM_EPS)

        y = pn[:, C_KAKI:C_KAKI + 128]
        fac = jnp.where(left, head_rsqrt(y), 1.0) * gka_ref[...]
        kaki_ref[0, tok, :] = rope_n(y * fac).astype(BF16)
        kb_ref[0, tok, :] = pn[:, C_KB:C_KB + 256].astype(BF16)
        y = pn[:, C_KC:C_KC + 128]
        kc_ref[0, tok, :] = rope_n(y * head_rsqrt(y) * gkc_ref[...]).astype(BF16)


def _inproj(x, mod_l, ln, wt, wn, tabs, gains):
    b, l, d = x.shape
    T = ATT_TILE
    tm = TOK_TILE
    nq = l // T
    nc = tm // T
    cos_t, sin_t, cos_n, sin_n = tabs
    gqa, gqc, gka, gkc = gains
    const2 = lambda bb, i: (0, 0)
    out_shape = (
        jax.ShapeDtypeStruct((b, nq, 128, 4 * T), BF16),
        jax.ShapeDtypeStruct((b, nq, 128, 4 * T), BF16),
        jax.ShapeDtypeStruct((b, nq, 128, 4 * T), BF16),
        jax.ShapeDtypeStruct((b, nq, 2, 128, 4 * T), BF16),
        jax.ShapeDtypeStruct((b, nq, 1, 4 * T), F32),
        jax.ShapeDtypeStruct((b, nq, 64, T), BF16),
        jax.ShapeDtypeStruct((b, nq, 256, T), BF16),
        jax.ShapeDtypeStruct((b, l // LANES, 128, LANES), BF16),
        jax.ShapeDtypeStruct((b, l, 128), BF16),
        jax.ShapeDtypeStruct((b, l, 256), BF16),
        jax.ShapeDtypeStruct((b, l, 128), BF16),
    )
    out_specs = (
        pl.BlockSpec((1, nc, 128, 4 * T), lambda bb, i: (bb, i, 0, 0)),
        pl.BlockSpec((1, nc, 128, 4 * T), lambda bb, i: (bb, i, 0, 0)),
        pl.BlockSpec((1, nc, 128, 4 * T), lambda bb, i: (bb, i, 0, 0)),
        pl.BlockSpec((1, nc, 2, 128, 4 * T), lambda bb, i: (bb, i, 0, 0, 0)),
        pl.BlockSpec((1, nc, 1, 4 * T), lambda bb, i: (bb, i, 0, 0)),
        pl.BlockSpec((1, nc, 64, T), lambda bb, i: (bb, i, 0, 0)),
        pl.BlockSpec((1, nc, 256, T), lambda bb, i: (bb, i, 0, 0)),
        pl.BlockSpec((1, tm // LANES, 128, LANES), lambda bb, i: (bb, i, 0, 0)),
        pl.BlockSpec((1, tm, 128), lambda bb, i: (bb, i, 0)),
        pl.BlockSpec((1, tm, 256), lambda bb, i: (bb, i, 0)),
        pl.BlockSpec((1, tm, 128), lambda bb, i: (bb, i, 0)),
    )
    in_specs = [
        pl.BlockSpec((1, tm, d), lambda bb, i: (bb, i, 0)),
        pl.BlockSpec((1, 6, d), lambda bb, i: (bb, 0, 0)),
        pl.BlockSpec((1, d), const2),
        pl.BlockSpec((R_TOTAL, d), const2),
        pl.BlockSpec((d, C_TOTAL), const2),
        pl.BlockSpec((32, tm), lambda bb, i: (0, i)),
        pl.BlockSpec((32, tm), lambda bb, i: (0, i)),
        pl.BlockSpec((tm, LANES), lambda bb, i: (i, 0)),
        pl.BlockSpec((tm, LANES), lambda bb, i: (i, 0)),
        pl.BlockSpec((HEAD_DIM, 1), const2),
        pl.BlockSpec((HEAD_DIM, 1), const2),
        pl.BlockSpec((1, LANES), const2),
        pl.BlockSpec((1, LANES), const2),
    ]
    return pl.pallas_call(
        functools.partial(_inproj_kernel, tile=T),
        grid=(b, l // tm),
        in_specs=in_specs,
        out_specs=out_specs,
        out_shape=out_shape,
        compiler_params=_cparams(("parallel", "arbitrary")),
        name="inproj",
    )(x, mod_l, ln, wt, wn, cos_t, sin_t, cos_n, sin_n, gqa, gqc, gka, gkc)


def _store_heads(o_ref, g_ref, heads_t, row):
    for p in range(len(heads_t) // 2):
        parts = []
        for o in heads_t[2 * p:2 * p + 2]:
            ms = jnp.mean(o * o, axis=0, keepdims=True)
            parts.append(o * lax.rsqrt(ms + NORM_EPS))
        ot = jnp.concatenate(parts, axis=0).T
        o_ref[row, :, 128 * p:128 * p + 128] = (ot * g_ref[:, 128 * p:128 * p + 128]).astype(BF16)


def _float_key(x):
    kb = pltpu.bitcast(x, I32)
    return jnp.where(kb < 0, INT_MIN - kb, kb)


def _key_float(k):
    return pltpu.bitcast(jnp.where(k < 0, INT_MIN - k, k), F32)


def _fold16(x, op):
    t = x.shape[0]
    return op(x.reshape(t // 16, 16, x.shape[1]), axis=0)


def _fold8(x, op):
    t = x.shape[0]
    return op(x.reshape(t // 8, 8, x.shape[1]), axis=0)


def _dsa_kernel(qi_ref, qa_ref, w_ref, kaki_ref, vat_ref, g_ref, o_ref,
                key_scr, s_scr, r_scr, p_scr, ib_scr, m_scr, l_scr, acc_scr, *, tile, topk, nb):
    T = tile
    K = topk
    NB = range(nb)
    i = pl.program_id(1)
    nk = i + 1
    rowk = lax.broadcasted_iota(I32, (T, T), 0)
    colq = lax.broadcasted_iota(I32, (T, T), 1)
    negk = -rowk

    def key_chunk(b, j):
        return kaki_ref[b, pl.ds(pl.multiple_of(j * T, T), T), :]

    def index_dots(j):
        for b in NB:
            kc = key_chunk(b, j)
            for h in range(N_IDX_HEADS):
                s_scr[b, h] = jnp.dot(kc, qi_ref[b, 0, :, h * T:(h + 1) * T],
                                      preferred_element_type=F32)

    def attn_dots(j):
        for b in NB:
            kc = key_chunk(b, j)
            for h in range(N_HEADS_A):
                r_scr[b, h] = jnp.dot(kc, qa_ref[b, 0, :, h * T:(h + 1) * T],
                                      preferred_element_type=F32)

    def score_chunk(j, nxt, diag, mm):
        scs = []
        for b in NB:
            sc = None
            for h in range(N_IDX_HEADS):
                r = jnp.maximum(s_scr[b, h], 0.0) * w_ref[b, 0, :, h * T:(h + 1) * T]
                sc = r if sc is None else sc + r
            scs.append(sc)
        index_dots(nxt)
        out = []
        for b in NB:
            sc = scs[b]
            key = jnp.where(sc == 0.0, negk - j * T, _float_key(sc))
            if diag:
                causal = rowk <= colq
                key_hi = jnp.where(causal, key, KEY_NEG_INF)
                key_lo = jnp.where(causal, key, INT_MAX)
            else:
                key_hi = key_lo = key
            key_scr[b, j] = key_hi
            mx8, mn8 = mm[b]
            out.append((jnp.maximum(mx8, _fold8(key_hi, jnp.max)),
                        jnp.minimum(mn8, _fold8(key_lo, jnp.min))))
        return tuple(out)

    index_dots(i)
    attn_dots(0)
    mm0 = tuple((jnp.full((8, T), INT_MIN, I32), jnp.full((8, T), INT_MAX, I32)) for _ in NB)
    mm = score_chunk(i, 0, True, mm0)
    mm = lax.fori_loop(0, i, lambda j, c: score_chunk(j, jnp.minimum(j + 1, i - 1), False, c), mm)

    def count_ge(mids):
        def body(p, accs):
            out = []
            for b in NB:
                a = _fold8(jnp.where(key_scr[b, 2 * p] >= mids[b], 1, 0), jnp.sum)
                c = _fold8(jnp.where(key_scr[b, 2 * p + 1] >= mids[b], 1, 0), jnp.sum)
                out.append(accs[b] + (a + c))
            return tuple(out)
        accs = lax.fori_loop(0, (nk + 1) // 2, body, tuple(jnp.zeros((8, T), I32) for _ in NB))
        return [jnp.sum(a, axis=0, keepdims=True) for a in accs]

    @pl.when(nk % 2 == 1)
    def _():
        for b in NB:
            key_scr[b, nk] = jnp.full((T, T), INT_MIN, I32)

    tq = i * T + lax.broadcasted_iota(I32, (1, T), 1)
    big = tq >= K
    zero_lo = -nk * T

    def pick(it, lo, hi, clo, chi):
        f = ((clo - K).astype(F32) + 0.5) / jnp.maximum(clo - chi, 1).astype(F32)
        frac = jnp.where(f < 0.25, jnp.maximum(2.0 * f, 1e-3),
                         jnp.where(f > 0.75, 1.0 - jnp.maximum(2.0 - 2.0 * f, 1e-3), 0.5))
        fmid = _float_key(_key_float(lo) * (1.0 - frac) + _key_float(hi) * frac)
        kmid = (lo >> 1) + (hi >> 1) + (lo & hi & 1)
        inside = jnp.where(fmid > lo, jnp.where(fmid < hi, it, FLOAT_PROBES), FLOAT_PROBES)
        mid = jnp.where(inside < FLOAT_PROBES, fmid, kmid)
        mid = jnp.where(lo == 0, jnp.where(hi > 1, 1, mid), mid)
        mid = jnp.where(lo < zero_lo, jnp.where(hi > zero_lo, zero_lo, mid), mid)
        return jnp.where(lo < 0, jnp.where(hi > 0, 0, mid), mid)

    def probes(n, st):
        _, it, brs = st
        for _ in range(n):
            mids = [pick(it, *brs[b]) for b in NB]
            cs = count_ge(mids)
            new = []
            for b in NB:
                lo, hi, clo, chi = brs[b]
                ge = cs[b] >= K
                new.append((jnp.where(ge, mids[b], lo), jnp.where(ge, hi, mids[b]),
                            jnp.where(ge, cs[b], clo), jnp.where(ge, chi, cs[b])))
            brs = tuple(new)
            it = it + 1
        nopen = 0.0
        for b in NB:
            lo, hi, clo, chi = brs[b]
            nopen = nopen + jnp.sum(jnp.where(clo == K, 0.0, jnp.where((hi - 1) == lo, 0.0, 1.0)))
        return nopen, it, brs

    brs0 = tuple((jnp.where(big, jnp.min(mm[b][1], axis=0, keepdims=True), KEY_NEG_INF),
                  jnp.where(big, jnp.max(mm[b][0], axis=0, keepdims=True) + 1, KEY_NEG_INF + 1),
                  tq + 1, jnp.zeros((1, T), I32)) for b in NB)
    st = probes(FIRST_PROBES, (0.0, jnp.zeros((1, T), I32), brs0))
    _, _, brs = lax.while_loop(lambda st: st[0] > 0.0,
                               functools.partial(probes, PROBES_PER_CHECK), st)

    ties = [jnp.where(big, jnp.where(brs[b][2] > K, 1.0, 0.0), 0.0) for b in NB]
    thrs = [jnp.maximum(brs[b][0], KEY_NEG_INF + 1) for b in NB]
    ntie = 0.0
    for b in NB:
        ib_scr[b] = jnp.full((1, T), IDX_BIG, I32)
        ntie = ntie + jnp.sum(ties[b])

    @pl.when(ntie > 0.0)
    def _():
        before = jnp.where(colq < rowk, 1.0, 0.0).astype(BF16)
        for b in NB:
            need = (K - brs[b][3]).astype(F32)

            def body(j, carry, b=b, need=need):
                run, ibm8 = carry
                eq = jnp.where(key_scr[b, j] == thrs[b], 1.0, 0.0)
                pc = jnp.dot(before, eq.astype(BF16), preferred_element_type=F32) + run
                idx1 = (rowk + (j * T + 1)).astype(F32)
                taken = jnp.where(pc < need, eq * idx1, 0.0)
                return (run + jnp.sum(_fold8(eq, jnp.sum), axis=0, keepdims=True),
                        jnp.maximum(ibm8, _fold8(taken, jnp.max)))

            _, ibm8 = lax.fori_loop(0, nk, body,
                                    (jnp.zeros((1, T), F32), jnp.zeros((8, T), F32)))
            ibm = jnp.max(ibm8, axis=0, keepdims=True).astype(I32)
            ib_scr[b] = jnp.where(ties[b] > 0.0, ibm, IDX_BIG)

    ibs = [ib_scr[b] for b in NB]

    m_scr[...] = jnp.full(m_scr.shape, NEG_BIG, F32)
    l_scr[...] = jnp.zeros(l_scr.shape, F32)
    acc_scr[...] = jnp.zeros(acc_scr.shape, F32)

    def attn_body(j, carry):
        m_new = {}
        for b in NB:
            t_el = jnp.where((rowk + j * T) < ibs[b], thrs[b], thrs[b] + 1)
            neg = jnp.where(key_scr[b, j] >= t_el, 0.0, NEG_BIG).astype(BF16)
            for h in range(N_HEADS_A):
                sb = r_scr[b, h].astype(BF16) + neg
                mc = jnp.max(_fold16(sb, jnp.max), axis=0, keepdims=True).astype(F32)
                m_new[b, h] = jnp.maximum(m_scr[b, h], mc)
            for h in range(N_HEADS_A):
                p_scr[b, h] = jnp.exp2((r_scr[b, h] - m_new[b, h]).astype(BF16) + neg)
        attn_dots(jnp.minimum(j + 1, i))
        for b in NB:
            vt1 = jnp.concatenate([vat_ref[b, j], jnp.ones((16, T), BF16)], axis=0)
            for h in range(N_HEADS_A):
                alpha = jnp.exp2(m_scr[b, h] - m_new[b, h])
                pv = jnp.dot(vt1, p_scr[b, h], preferred_element_type=F32)
                l_scr[b, h] = alpha * l_scr[b, h] + pv[HEAD_DIM:HEAD_DIM + 1]
                acc_scr[b, h] = alpha * acc_scr[b, h] + pv[0:HEAD_DIM]
                m_scr[b, h] = m_new[b, h]
        return carry

    lax.fori_loop(0, nk, attn_body, 0)
    for b in NB:
        _store_heads(o_ref, g_ref, [acc_scr[b, h] / l_scr[b, h] for h in range(N_HEADS_A)], b)


def _dsa(qi, qa, w, kaki, vat, g, topk):
    b, nq, _, _ = qa.shape
    T = ATT_TILE
    l = nq * T
    nb = _rows_per_step(b)
    return pl.pallas_call(
        functools.partial(_dsa_kernel, tile=T, topk=topk, nb=nb),
        grid=(b // nb, nq),
        in_specs=[
            pl.BlockSpec((nb, 1, 128, 4 * T), lambda bb, i: (bb, i, 0, 0)),
            pl.BlockSpec((nb, 1, 128, 4 * T), lambda bb, i: (bb, i, 0, 0)),
            pl.BlockSpec((nb, 1, 1, 4 * T), lambda bb, i: (bb, i, 0, 0)),
            pl.BlockSpec((nb, l, 128), lambda bb, i: (bb, 0, 0)),
            pl.BlockSpec((nb, nq, 64, T), lambda bb, i: (bb, 0, 0, 0)),
            pl.BlockSpec((1, 256), lambda bb, i: (0, 0)),
        ],
        out_specs=pl.BlockSpec((nb, T, 256), lambda bb, i: (bb, i, 0)),
        out_shape=jax.ShapeDtypeStruct((b, l, 256), BF16),
        scratch_shapes=[
            pltpu.VMEM((nb, nq + nq % 2, T, T), I32),
            pltpu.VMEM((nb, N_HEADS_A, T, T), F32),
            pltpu.VMEM((nb, N_HEADS_A, T, T), F32),
            pltpu.VMEM((nb, N_HEADS_A, T, T), BF16),
            pltpu.VMEM((nb, 1, T), I32),
            pltpu.VMEM((nb, N_HEADS_A, 1, T), F32),
            pltpu.VMEM((nb, N_HEADS_A, 1, T), F32),
            pltpu.VMEM((nb, N_HEADS_A, HEAD_DIM, T), F32),
        ],
        compiler_params=_cparams(("parallel", "arbitrary")),
        name="dsa",
    )(qi, qa, w, kaki, vat, g)


def _sb_kernel(qb_ref, kb_ref, vbt_ref, g_ref, o_ref, run_scr, acc_scr, z_scr, e_scr, c_scr, hl_scr,
               *, tile, nb):
    T = tile
    NB = range(nb)
    i = pl.program_id(1)
    rowk = lax.broadcasted_iota(I32, (T, T), 0)
    colq = lax.broadcasted_iota(I32, (T, T), 1)
    strict = rowk < colq
    tmat = jnp.where(colq > rowk, 1.0, 0.0).astype(BF16)
    run_scr[...] = jnp.zeros(run_scr.shape, F32)
    acc_scr[...] = jnp.zeros(acc_scr.shape, F32)

    def scores(j):
        row0 = pl.multiple_of(j * T, T)
        for b in NB:
            for h in range(N_HEADS_B):
                p = h // 2
                kc = kb_ref[b, pl.ds(row0, T), 128 * p:128 * p + 128]
                z_scr[b, h] = jnp.dot(kc, qb_ref[b, 0, :, h * T:(h + 1) * T],
                                      preferred_element_type=F32)

    def chunk(j, diag):
        tots = {}
        for b in NB:
            for h in range(N_HEADS_B):
                z = z_scr[b, h]
                nz = -z
                lk = jnp.minimum(nz, 0.0) - jnp.log2(1.0 + jnp.exp2(jnp.minimum(z, nz)))
                if diag:
                    lk = jnp.where(strict, lk, 0.0)
                hl_scr[b, h] = lk.astype(BF16)
                e_scr[b, h] = z + lk
                c_scr[b, h] = jnp.dot(tmat, hl_scr[b, h], preferred_element_type=F32)
                tots[b, h] = c_scr[b, h][0:1, :] + lk[0:1, :]
        scores(jnp.maximum(j - 1, 0))
        for b in NB:
            for h in range(N_HEADS_B):
                a = jnp.exp2(e_scr[b, h] + c_scr[b, h] + run_scr[b, h])
                if diag:
                    a = jnp.where(strict, a, 0.0)
                acc_scr[b, h] = acc_scr[b, h] + jnp.dot(vbt_ref[b, j, 64 * h:64 * h + 64, :],
                                                        a.astype(BF16), preferred_element_type=F32)
                run_scr[b, h] = run_scr[b, h] + tots[b, h]

    scores(i)
    chunk(i, True)

    def body(jj, carry):
        chunk(i - 1 - jj, False)
        return carry

    lax.fori_loop(0, i, body, 0)
    for b in NB:
        _store_heads(o_ref, g_ref, [acc_scr[b, h] for h in range(N_HEADS_B)], b)


def _sb(qb, kb, vbt, g):
    b, nq, _, _ = qb.shape
    T = ATT_TILE
    l = nq * T
    nb = _rows_per_step(b)
    return pl.pallas_call(
        functools.partial(_sb_kernel, tile=T, nb=nb),
        grid=(b // nb, nq),
        in_specs=[
            pl.BlockSpec((nb, 1, 128, 4 * T), lambda bb, i: (bb, i, 0, 0)),
            pl.BlockSpec((nb, l, 256), lambda bb, i: (bb, 0, 0)),
            pl.BlockSpec((nb, nq, 256, T), lambda bb, i: (bb, 0, 0, 0)),
            pl.BlockSpec((1, 256), lambda bb, i: (0, 0)),
        ],
        out_specs=pl.BlockSpec((nb, T, 256), lambda bb, i: (bb, i, 0)),
        out_shape=jax.ShapeDtypeStruct((b, l, 256), BF16),
        scratch_shapes=[
            pltpu.VMEM((nb, N_HEADS_B, 1, T), F32),
            pltpu.VMEM((nb, N_HEADS_B, HEAD_DIM, T), F32),
            pltpu.VMEM((nb, N_HEADS_B, T, T), F32),
            pltpu.VMEM((nb, N_HEADS_B, T, T), F32),
            pltpu.VMEM((nb, N_HEADS_B, T, T), F32),
            pltpu.VMEM((nb, N_HEADS_B, T, T), BF16),
        ],
        compiler_params=_cparams(("parallel", "arbitrary")),
        name="stickbreak",
    )(qb, kb, vbt, g)


def _swa_kernel(sink_ref, qc_ref, kc_ref, vct_ref, g_ref, o_ref, s_scr, *, tile, nb):
    T = tile
    NKEY = T + WINDOW
    NB = range(nb)
    i = pl.program_id(1)
    start = pl.multiple_of(jnp.maximum(i * T - WINDOW, 0), WINDOW)
    kidx = start + lax.broadcasted_iota(I32, (NKEY, T), 0)
    tq = i * T + lax.broadcasted_iota(I32, (NKEY, T), 1)
    d = tq - kidx
    band = (d >= 0) & (d < WINDOW)
    c0 = start // WINDOW
    per_kv = N_HEADS_C // N_KV_C
    for b in NB:
        kc = kc_ref[b, pl.ds(start, NKEY), :]
        for h in range(N_HEADS_C):
            s_scr[b, h] = jnp.dot(
                kc, qc_ref[b, 0, h // per_kv, :, (h % per_kv) * T:(h % per_kv + 1) * T],
                preferred_element_type=F32)
    neg = jnp.where(band, 0.0, NEG_BIG).astype(BF16)
    ones = jnp.ones((16, WINDOW), BF16)
    for b in NB:
        outs = []
        for h in range(N_HEADS_C):
            g = h // per_kv
            sink = sink_ref[h] * LOG2E
            sb = s_scr[b, h].astype(BF16) + neg
            m = jnp.maximum(jnp.max(_fold16(sb, jnp.max), axis=0, keepdims=True).astype(F32), sink)
            p = jnp.exp2((s_scr[b, h] - m).astype(BF16) + neg)
            o = None
            for c in range(NKEY // WINDOW):
                vt1 = jnp.concatenate([vct_ref[b, c0 + c, 64 * g:64 * g + 64, :], ones], axis=0)
                part = jnp.dot(vt1, p[c * WINDOW:(c + 1) * WINDOW, :],
                               preferred_element_type=F32)
                o = part if o is None else o + part
            den = o[HEAD_DIM:HEAD_DIM + 1] + jnp.exp2(sink - m)
            outs.append(o[0:HEAD_DIM] / den)
        _store_heads(o_ref, g_ref, outs, b)


def _swa(sinks, qc, kc, vct, g):
    b, nq = qc.shape[:2]
    T = ATT_TILE
    l = nq * T
    nb = _rows_per_step(b)
    return pl.pallas_call(
        functools.partial(_swa_kernel, tile=T, nb=nb),
        grid=(b // nb, nq),
        in_specs=[
            pl.BlockSpec(memory_space=pltpu.SMEM),
            pl.BlockSpec((nb, 1, 2, 128, 4 * T), lambda bb, i: (bb, i, 0, 0, 0)),
            pl.BlockSpec((nb, l, 128), lambda bb, i: (bb, 0, 0)),
            pl.BlockSpec((nb, l // LANES, 128, LANES), lambda bb, i: (bb, 0, 0, 0)),
            pl.BlockSpec((1, 512), lambda bb, i: (0, 0)),
        ],
        out_specs=pl.BlockSpec((nb, T, 512), lambda bb, i: (bb, i, 0)),
        out_shape=jax.ShapeDtypeStruct((b, l, 512), BF16),
        scratch_shapes=[pltpu.VMEM((nb, N_HEADS_C, T + WINDOW, T), F32)],
        compiler_params=_cparams(("parallel", "arbitrary")),
        name="swa",
    )(sinks, qc, kc, vct, g)


def _ffn_chunks(d_ff):
    chunks, c0 = [], 0
    while c0 < d_ff:
        cw = min(1024, d_ff - c0)
        chunks.append((c0, cw))
        c0 += cw
    return chunks


def _merge_ffn_kernel(x_ref, oa_ref, ob_ref, oc_ref, mod_ref, ln_ref, woa_ref, wob_ref, woc_ref,
                      wg_ref, wu_ref, wd_ref, out_ref):
    x = x_ref[0]
    y = (jnp.dot(oa_ref[0], woa_ref[...], preferred_element_type=F32)
         + jnp.dot(ob_ref[0], wob_ref[...], preferred_element_type=F32)
         + jnp.dot(oc_ref[0], woc_ref[...], preferred_element_type=F32))
    x1 = x + mod_ref[0, 2:3, :] * y
    ms = jnp.mean(x1 * x1, axis=-1, keepdims=True)
    h = x1 * lax.rsqrt(ms + NORM_EPS) * ln_ref[...]
    hb = (h * (1.0 + mod_ref[0, 4:5, :]) + mod_ref[0, 3:4, :]).astype(BF16)
    acc = None
    for c0, cw in _ffn_chunks(wg_ref.shape[1]):
        gt = jnp.dot(hb, wg_ref[:, c0:c0 + cw], preferred_element_type=F32)
        up = jnp.dot(hb, wu_ref[:, c0:c0 + cw], preferred_element_type=F32)
        act = (gt * jax.nn.sigmoid(gt) * up).astype(BF16)
        part = jnp.dot(act, wd_ref[c0:c0 + cw, :], preferred_element_type=F32)
        acc = part if acc is None else acc + part
    out_ref[0] = x1 + mod_ref[0, 5:6, :] * acc


def _merge_ffn(x, oa, ob, oc, mod_l, ln, woa, wob, woc, wg, wu, wd):
    b, l, d = x.shape
    tm = TOK_TILE
    dff = wg.shape[1]
    const2 = lambda bb, i: (0, 0)
    once = pl.Buffered(1)
    return pl.pallas_call(
        _merge_ffn_kernel,
        grid=(b, l // tm),
        in_specs=[
            pl.BlockSpec((1, tm, d), lambda bb, i: (bb, i, 0)),
            pl.BlockSpec((1, tm, 256), lambda bb, i: (bb, i, 0)),
            pl.BlockSpec((1, tm, 256), lambda bb, i: (bb, i, 0)),
            pl.BlockSpec((1, tm, 512), lambda bb, i: (bb, i, 0)),
            pl.BlockSpec((1, 6, d), lambda bb, i: (bb, 0, 0)),
            pl.BlockSpec((1, d), const2),
            pl.BlockSpec((256, d), const2, pipeline_mode=once),
            pl.BlockSpec((256, d), const2, pipeline_mode=once),
            pl.BlockSpec((512, d), const2, pipeline_mode=once),
            pl.BlockSpec((d, dff), const2, pipeline_mode=once),
            pl.BlockSpec((d, dff), const2, pipeline_mode=once),
            pl.BlockSpec((dff, d), const2, pipeline_mode=once),
        ],
        out_specs=pl.BlockSpec((1, tm, d), lambda bb, i: (bb, i, 0)),
        out_shape=jax.ShapeDtypeStruct((b, l, d), F32),
        compiler_params=_cparams(("parallel", "arbitrary")),
        name="merge_ffn",
    )(x, oa, ob, oc, mod_l, ln, woa, wob, woc, wg, wu, wd)


def _rope_tables(l):
    inv = 1.0 / (ROPE_THETA ** (jnp.arange(0, HEAD_DIM, 2, dtype=F32) / HEAD_DIM))
    ang = jnp.arange(l, dtype=F32)[:, None] * inv[None, :]
    cos, sin = jnp.cos(ang), jnp.sin(ang)
    cos_n = jnp.tile(cos, (1, 4))
    sin_n = jnp.concatenate([-sin, sin, -sin, sin], axis=1)
    return cos.T, sin.T, cos_n, sin_n


def _split_w_in(w_in):
    widths = (256, 64, 64, 256, 64, 4, 256, 256, 256, 512, 128, 128)
    offs = [0]
    for w in widths:
        offs.append(offs[-1] + w)
    qa, ka, va, qi, ki, wi, qb, kb, vb, qc, kc, vc = [w_in[:, :, offs[k]:offs[k + 1]]
                                                      for k in range(len(widths))]
    idx_scale = float((N_IDX_HEADS * HEAD_DIM) ** -0.5)
    att_scale = float(HEAD_DIM ** -0.5)
    pad = jnp.zeros(wi.shape[:2] + (R_TOTAL - R_W - N_IDX_HEADS,), w_in.dtype)
    wt = jnp.concatenate([qa, qi, qb * (att_scale * LOG2E), qc, va, vb, vc, wi * idx_scale, pad],
                         axis=2)
    wt = jnp.swapaxes(wt, 1, 2).astype(BF16)
    wn = jnp.concatenate([ka, ki, kb, kc], axis=2).astype(BF16)
    return wt, wn


@jax.jit
def kernel(x, c, ln1, ln2, w_mod, b_mod, w_in, qn_a, kn_a, qn_c, kn_c, sinks, g_out, w_o,
           w_gate, w_up, w_down):
    depth = w_in.shape[0]
    b, l, d = x.shape
    assert l % TOK_TILE == 0 and l >= ATT_TILE + WINDOW
    topk = min(TOPK_MAX, l // 4)
    att_scale = float(HEAD_DIM ** -0.5)

    tabs = _rope_tables(l)
    mod = _modulation(c, w_mod, b_mod)
    wt, wn = _split_w_in(w_in)
    w_o_b = w_o.astype(BF16)
    w_g_b, w_u_b, w_d_b = w_gate.astype(BF16), w_up.astype(BF16), w_down.astype(BF16)
    ones = jnp.ones((HEAD_DIM,), F32)

    for li in range(depth):
        gains = ((qn_a[li] * (att_scale * LOG2E)).reshape(HEAD_DIM, 1),
                 (qn_c[li] * (att_scale * LOG2E)).reshape(HEAD_DIM, 1),
                 jnp.concatenate([kn_a[li], ones]).reshape(1, LANES),
                 jnp.concatenate([kn_c[li], kn_c[li]]).reshape(1, LANES))
        (qa, qi, qb, qc, w, vat, vbt, vct, kaki, kb, kc) = _inproj(
            x, mod[li], ln1[li].reshape(1, d), wt[li], wn[li], tabs, gains)
        g = g_out[li].reshape(1, -1)
        oa = _dsa(qi, qa, w, kaki, vat, g[:, 0:256], topk)
        ob = _sb(qb, kb, vbt, g[:, 256:512])
        oc = _swa(sinks[li], qc, kc, vct, g[:, 512:1024])
        x = _merge_ffn(x, oa, ob, oc, mod[li], ln2[li].reshape(1, d),
                       w_o_b[li, 0:256], w_o_b[li, 256:512], w_o_b[li, 512:1024],
                       w_g_b[li], w_u_b[li], w_d_b[li])
    return x
```

```python
import functools

import jax
import jax.numpy as jnp
from jax import lax
from jax.experimental import pallas as pl
from jax.experimental.pallas import tpu as pltpu

F32 = jnp.float32
BF16 = jnp.bfloat16
I32 = jnp.int32

HEAD_DIM = 64
N_HEADS_A = 4
N_HEADS_B = 4
N_HEADS_C = 8
N_KV_C = 2
N_IDX_HEADS = 4
TOPK_MAX = 256
WINDOW = 128
ROPE_THETA = 10000.0
NORM_EPS = 1e-6
LANES = 128
ATT_TILE = 256
TOK_TILE = 512
INPROJ_TILE = 1024
VMEM_LIMIT = 48 * 1024 * 1024

NEG_BIG = -1e30
INT_MIN = -2 ** 31
INT_MAX = 2 ** 31 - 1
KEY_NEG_INF = -0x7F800000
IDX_BIG = 2 ** 30
LOG2E = 1.4426950408889634
FIRST_PROBES = 14
PROBES_PER_CHECK = 2
FLOAT_PROBES = 30

R_QA, R_QI, R_QB, R_QC = 0, 256, 512, 768
R_VA, R_VB, R_VC, R_W = 1280, 1344, 1600, 1728
R_TOTAL = 1744
C_KAKI, C_KB, C_KC = 0, 128, 384
C_TOTAL = 512


def _cparams(sem):
    return pltpu.CompilerParams(dimension_semantics=sem, vmem_limit_bytes=VMEM_LIMIT)


def _rows_per_step(b):
    return 4 if b % 4 == 0 else (2 if b % 2 == 0 else 1)


def _mod_kernel(c_ref, w_ref, b_ref, o_ref):
    c = c_ref[...]
    ca = (c * jax.nn.sigmoid(c)).astype(BF16)
    y = jnp.dot(ca, w_ref[0].astype(BF16), preferred_element_type=F32)
    o_ref[0, 0] = y + b_ref[0, 0]


def _modulation(c, w_mod, b_mod):
    depth, d, _ = w_mod.shape
    b = c.shape[0]
    out = pl.pallas_call(
        _mod_kernel,
        grid=(depth, 6),
        in_specs=[
            pl.BlockSpec((b, d), lambda l, k: (0, 0)),
            pl.BlockSpec((1, d, d), lambda l, k: (l, 0, k)),
            pl.BlockSpec((1, 1, 1, d), lambda l, k: (l, k, 0, 0)),
        ],
        out_specs=pl.BlockSpec((1, 1, b, d), lambda l, k: (l, k, 0, 0)),
        out_shape=jax.ShapeDtypeStruct((depth, 6, b, d), F32),
        compiler_params=_cparams(("arbitrary", "arbitrary")),
        name="modulation",
    )(c, w_mod, b_mod.reshape(depth, 6, 1, d))
    return jnp.transpose(out, (0, 2, 1, 3))


def _inproj_kernel(x_ref, mod_ref, ln_ref, wt_ref, wn_ref, cost_ref, sint_ref, cosn_ref, sinn_ref,
                   gqa_ref, gqc_ref, gka_ref, gkc_ref,
                   qa_ref, qi_ref, qb_ref, qc_ref, w_ref, vat_ref, vbt_ref, vct_ref,
                   kaki_ref, kb_ref, kc_ref, *, tile):
    T = tile
    nc = x_ref.shape[1] // T
    gqa = gqa_ref[...]
    gqc = gqc_ref[...]
    lane = lax.broadcasted_iota(I32, (T, LANES), 1)
    left = lane < HEAD_DIM
    first = (lane & 32) == 0
    zero_half = jnp.zeros((HEAD_DIM, T), BF16)
    per_kv = N_HEADS_C // N_KV_C

    for c in range(nc):
        tok = slice(c * T, (c + 1) * T)
        x = x_ref[0, tok, :]
        ms = jnp.mean(x * x, axis=-1, keepdims=True)
        h = x * lax.rsqrt(ms + NORM_EPS) * ln_ref[...]
        h = h * (1.0 + mod_ref[0, 1:2, :]) + mod_ref[0, 0:1, :]
        hb = h.astype(BF16)
        pt = lax.dot_general(wt_ref[...], hb, (((1,), (1,)), ((), ())),
                             preferred_element_type=F32)
        pn = jnp.dot(hb, wn_ref[...], preferred_element_type=F32)

        cos_t = cost_ref[:, tok]
        sin_t = sint_ref[:, tok]

        def rope_t(y):
            y1, y2 = y[0:32], y[32:64]
            return jnp.concatenate([y1 * cos_t - y2 * sin_t, y2 * cos_t + y1 * sin_t], axis=0)

        def rms_t(y, g):
            m = jnp.mean(y * y, axis=0, keepdims=True)
            return y * lax.rsqrt(m + NORM_EPS) * g

        def put_padded(ref, lead, y, half, col):
            idx = (0, c) + lead
            ref[idx + (slice(64 * half, 64 * half + 64), slice(col * T, (col + 1) * T))] = \
                y.astype(BF16)
            ref[idx + (slice(64 * (1 - half), 64 * (1 - half) + 64),
                       slice(col * T, (col + 1) * T))] = zero_half

        for hh in range(N_HEADS_A):
            put_padded(qa_ref, (), rope_t(rms_t(pt[R_QA + 64 * hh:R_QA + 64 * hh + 64], gqa)), 0, hh)
        for hh in range(N_IDX_HEADS):
            put_padded(qi_ref, (), rope_t(pt[R_QI + 64 * hh:R_QI + 64 * hh + 64]), 1, hh)
        for hh in range(N_HEADS_B):
            put_padded(qb_ref, (), pt[R_QB + 64 * hh:R_QB + 64 * hh + 64], hh % 2, hh)
        for hh in range(N_HEADS_C):
            g = hh // per_kv
            put_padded(qc_ref, (g,), rope_t(rms_t(pt[R_QC + 64 * hh:R_QC + 64 * hh + 64], gqc)),
                       g, hh % per_kv)
        vat_ref[0, c] = pt[R_VA:R_VA + 64].astype(BF16)
        vbt_ref[0, c] = pt[R_VB:R_VB + 256].astype(BF16)
        for hh in range(N_IDX_HEADS):
            w_ref[0, c, :, hh * T:(hh + 1) * T] = pt[R_W + hh:R_W + hh + 1]
        for cc in range(T // LANES):
            vct_ref[0, c * (T // LANES) + cc] = \
                pt[R_VC:R_VC + 128, cc * LANES:(cc + 1) * LANES].astype(BF16)

        cos_n = cosn_ref[tok, :]
        sin_n = sinn_ref[tok, :]

        def rope_n(y):
            rot = jnp.where(first, pltpu.roll(y, 96, 1), pltpu.roll(y, 32, 1))
            return y * cos_n + rot * sin_n

        def head_rsqrt(y):
            sq = y * y
            s0 = jnp.sum(jnp.where(left, sq, 0.0), axis=-1, keepdims=True)
            s1 = jnp.sum(jnp.where(left, 0.0, sq), axis=-1, keepdims=True)
            return lax.rsqrt(jnp.where(left, s0, s1) * (1.0 / HEAD_DIM) + NORM_EPS)

        y = pn[:, C_KAKI:C_KAKI + 128]
        fac = jnp.where(left, head_rsqrt(y), 1.0) * gka_ref[...]
        kaki_ref[0, tok, :] = rope_n(y * fac).astype(BF16)
        kb_ref[0, tok, :] = pn[:, C_KB:C_KB + 256].astype(BF16)
        y = pn[:, C_KC:C_KC + 128]
        kc_ref[0, tok, :] = rope_n(y * head_rsqrt(y) * gkc_ref[...]).astype(BF16)


def _inproj(x, mod_l, ln, wt, wn, tabs, gains):
    b, l, d = x.shape
    T = ATT_TILE
    tm = INPROJ_TILE if l % INPROJ_TILE == 0 else TOK_TILE
    nq = l // T
    nc = tm // T
    cos_t, sin_t, cos_n, sin_n = tabs
    gqa, gqc, gka, gkc = gains
    const2 = lambda bb, i: (0, 0)
    out_shape = (
        jax.ShapeDtypeStruct((b, nq, 128, 4 * T), BF16),
        jax.ShapeDtypeStruct((b, nq, 128, 4 * T), BF16),
        jax.ShapeDtypeStruct((b, nq, 128, 4 * T), BF16),
        jax.ShapeDtypeStruct((b, nq, 2, 128, 4 * T), BF16),
        jax.ShapeDtypeStruct((b, nq, 1, 4 * T), F32),
        jax.ShapeDtypeStruct((b, nq, 64, T), BF16),
        jax.ShapeDtypeStruct((b, nq, 256, T), BF16),
        jax.ShapeDtypeStruct((b, l // LANES, 128, LANES), BF16),
        jax.ShapeDtypeStruct((b, l, 128), BF16),
        jax.ShapeDtypeStruct((b, l, 256), BF16),
        jax.ShapeDtypeStruct((b, l, 128), BF16),
    )
    out_specs = (
        pl.BlockSpec((1, nc, 128, 4 * T), lambda bb, i: (bb, i, 0, 0)),
        pl.BlockSpec((1, nc, 128, 4 * T), lambda bb, i: (bb, i, 0, 0)),
        pl.BlockSpec((1, nc, 128, 4 * T), lambda bb, i: (bb, i, 0, 0)),
        pl.BlockSpec((1, nc, 2, 128, 4 * T), lambda bb, i: (bb, i, 0, 0, 0)),
        pl.BlockSpec((1, nc, 1, 4 * T), lambda bb, i: (bb, i, 0, 0)),
        pl.BlockSpec((1, nc, 64, T), lambda bb, i: (bb, i, 0, 0)),
        pl.BlockSpec((1, nc, 256, T), lambda bb, i: (bb, i, 0, 0)),
        pl.BlockSpec((1, tm // LANES, 128, LANES), lambda bb, i: (bb, i, 0, 0)),
        pl.BlockSpec((1, tm, 128), lambda bb, i: (bb, i, 0)),
        pl.BlockSpec((1, tm, 256), lambda bb, i: (bb, i, 0)),
        pl.BlockSpec((1, tm, 128), lambda bb, i: (bb, i, 0)),
    )
    in_specs = [
        pl.BlockSpec((1, tm, d), lambda bb, i: (bb, i, 0)),
        pl.BlockSpec((1, 6, d), lambda bb, i: (bb, 0, 0)),
        pl.BlockSpec((1, d), const2),
        pl.BlockSpec((R_TOTAL, d), const2),
        pl.BlockSpec((d, C_TOTAL), const2),
        pl.BlockSpec((32, tm), lambda bb, i: (0, i)),
        pl.BlockSpec((32, tm), lambda bb, i: (0, i)),
        pl.BlockSpec((tm, LANES), lambda bb, i: (i, 0)),
        pl.BlockSpec((tm, LANES), lambda bb, i: (i, 0)),
        pl.BlockSpec((HEAD_DIM, 1), const2),
        pl.BlockSpec((HEAD_DIM, 1), const2),
        pl.BlockSpec((1, LANES), const2),
        pl.BlockSpec((1, LANES), const2),
    ]
    return pl.pallas_call(
        functools.partial(_inproj_kernel, tile=T),
        grid=(b, l // tm),
        in_specs=in_specs,
        out_specs=out_specs,
        out_shape=out_shape,
        compiler_params=_cparams(("parallel", "arbitrary")),
        name="inproj",
    )(x, mod_l, ln, wt, wn, cos_t, sin_t, cos_n, sin_n, gqa, gqc, gka, gkc)


def _store_heads(o_ref, g_ref, heads_t, row):
    for p in range(len(heads_t) // 2):
        parts = []
        for o in heads_t[2 * p:2 * p + 2]:
            ms = jnp.mean(o * o, axis=0, keepdims=True)
            parts.append(o * lax.rsqrt(ms + NORM_EPS))
        ot = jnp.concatenate(parts, axis=0).T
        o_ref[row, :, 128 * p:128 * p + 128] = (ot * g_ref[:, 128 * p:128 * p + 128]).astype(BF16)


def _float_key(x):
    kb = pltpu.bitcast(x, I32)
    return jnp.where(kb < 0, INT_MIN - kb, kb)


def _key_float(k):
    return pltpu.bitcast(jnp.where(k < 0, INT_MIN - k, k), F32)


def _fold16(x, op):
    t = x.shape[0]
    return op(x.reshape(t // 16, 16, x.shape[1]), axis=0)


def _fold8(x, op):
    t = x.shape[0]
    return op(x.reshape(t // 8, 8, x.shape[1]), axis=0)


def _dsa_kernel(qi_ref, qa_ref, w_ref, kaki_ref, vat_ref, g_ref, o_ref,
                key_scr, s_scr, r_scr, p_scr, ib_scr, m_scr, l_scr, acc_scr, *, tile, topk, nb):
    T = tile
    K = topk
    NB = range(nb)
    i = pl.program_id(1)
    nk = i + 1
    rowk = lax.broadcasted_iota(I32, (T, T), 0)
    colq = lax.broadcasted_iota(I32, (T, T), 1)
    negk = -rowk

    def key_chunk(b, j):
        return kaki_ref[b, pl.ds(pl.multiple_of(j * T, T), T), :]

    def index_dots(j):
        for b in NB:
            kc = key_chunk(b, j)
            for h in range(N_IDX_HEADS):
                s_scr[b, h] = jnp.dot(kc, qi_ref[b, 0, :, h * T:(h + 1) * T],
                                      preferred_element_type=F32)

    def attn_dots(j):
        for b in NB:
            kc = key_chunk(b, j)
            for h in range(N_HEADS_A):
                r_scr[b, h] = jnp.dot(kc, qa_ref[b, 0, :, h * T:(h + 1) * T],
                                      preferred_element_type=F32)

    def score_chunk(j, nxt, diag, mm):
        scs = []
        for b in NB:
            sc = None
            for h in range(N_IDX_HEADS):
                r = jnp.maximum(s_scr[b, h], 0.0) * w_ref[b, 0, :, h * T:(h + 1) * T]
                sc = r if sc is None else sc + r
            scs.append(sc)
        index_dots(nxt)
        out = []
        for b in NB:
            sc = scs[b]
            key = jnp.where(sc == 0.0, negk - j * T, _float_key(sc))
            if diag:
                causal = rowk <= colq
                key_hi = jnp.where(causal, key, KEY_NEG_INF)
                key_lo = jnp.where(causal, key, INT_MAX)
            else:
                key_hi = key_lo = key
            key_scr[b, j] = key_hi
            mx8, mn8 = mm[b]
            out.append((jnp.maximum(mx8, _fold8(key_hi, jnp.max)),
                        jnp.minimum(mn8, _fold8(key_lo, jnp.min))))
        return tuple(out)

    index_dots(i)
    attn_dots(0)
    mm0 = tuple((jnp.full((8, T), INT_MIN, I32), jnp.full((8, T), INT_MAX, I32)) for _ in NB)
    mm = score_chunk(i, 0, True, mm0)
    mm = lax.fori_loop(0, i, lambda j, c: score_chunk(j, jnp.minimum(j + 1, i - 1), False, c), mm)

    def count_ge(mids):
        def body(p, accs):
            out = []
            for b in NB:
                a = _fold8(jnp.where(key_scr[b, 2 * p] >= mids[b], 1, 0), jnp.sum)
                c = _fold8(jnp.where(key_scr[b, 2 * p + 1] >= mids[b], 1, 0), jnp.sum)
                out.append(accs[b] + (a + c))
            return tuple(out)
        accs = lax.fori_loop(0, (nk + 1) // 2, body, tuple(jnp.zeros((8, T), I32) for _ in NB))
        return [jnp.sum(a, axis=0, keepdims=True) for a in accs]

    @pl.when(nk % 2 == 1)
    def _():
        for b in NB:
            key_scr[b, nk] = jnp.full((T, T), INT_MIN, I32)

    tq = i * T + lax.broadcasted_iota(I32, (1, T), 1)
    big = tq >= K
    zero_lo = -nk * T

    def pick(it, lo, hi, clo, chi):
        f = ((clo - K).astype(F32) + 0.5) / jnp.maximum(clo - chi, 1).astype(F32)
        frac = jnp.where(f < 0.25, jnp.maximum(2.0 * f, 1e-3),
                         jnp.where(f > 0.75, 1.0 - jnp.maximum(2.0 - 2.0 * f, 1e-3), 0.5))
        fmid = _float_key(_key_float(lo) * (1.0 - frac) + _key_float(hi) * frac)
        kmid = (lo >> 1) + (hi >> 1) + (lo & hi & 1)
        inside = jnp.where(fmid > lo, jnp.where(fmid < hi, it, FLOAT_PROBES), FLOAT_PROBES)
        mid = jnp.where(inside < FLOAT_PROBES, fmid, kmid)
        mid = jnp.where(lo == 0, jnp.where(hi > 1, 1, mid), mid)
        mid = jnp.where(lo < zero_lo, jnp.where(hi > zero_lo, zero_lo, mid), mid)
        return jnp.where(lo < 0, jnp.where(hi > 0, 0, mid), mid)

    def probes(n, st):
        _, it, brs = st
        for _ in range(n):
            mids = [pick(it, *brs[b]) for b in NB]
            cs = count_ge(mids)
            new = []
            for b in NB:
                lo, hi, clo, chi = brs[b]
                ge = cs[b] >= K
                new.append((jnp.where(ge, mids[b], lo), jnp.where(ge, hi, mids[b]),
                            jnp.where(ge, cs[b], clo), jnp.where(ge, chi, cs[b])))
            brs = tuple(new)
            it = it + 1
        nopen = 0.0
        for b in NB:
            lo, hi, clo, chi = brs[b]
            nopen = nopen + jnp.sum(jnp.where(clo == K, 0.0, jnp.where((hi - 1) == lo, 0.0, 1.0)))
        return nopen, it, brs

    brs0 = tuple((jnp.where(big, jnp.min(mm[b][1], axis=0, keepdims=True), KEY_NEG_INF),
                  jnp.where(big, jnp.max(mm[b][0], axis=0, keepdims=True) + 1, KEY_NEG_INF + 1),
                  tq + 1, jnp.zeros((1, T), I32)) for b in NB)
    st = probes(FIRST_PROBES, (0.0, jnp.zeros((1, T), I32), brs0))
    _, _, brs = lax.while_loop(lambda st: st[0] > 0.0,
                               functools.partial(probes, PROBES_PER_CHECK), st)

    ties = [jnp.where(big, jnp.where(brs[b][2] > K, 1.0, 0.0), 0.0) for b in NB]
    thrs = [jnp.maximum(brs[b][0], KEY_NEG_INF + 1) for b in NB]
    ntie = 0.0
    for b in NB:
        ib_scr[b] = jnp.full((1, T), IDX_BIG, I32)
        ntie = ntie + jnp.sum(ties[b])

    @pl.when(ntie > 0.0)
    def _():
        before = jnp.where(colq < rowk, 1.0, 0.0).astype(BF16)
        for b in NB:
            need = (K - brs[b][3]).astype(F32)

            def body(j, carry, b=b, need=need):
                run, ibm8 = carry
                eq = jnp.where(key_scr[b, j] == thrs[b], 1.0, 0.0)
                pc = jnp.dot(before, eq.astype(BF16), preferred_element_type=F32) + run
                idx1 = (rowk + (j * T + 1)).astype(F32)
                taken = jnp.where(pc < need, eq * idx1, 0.0)
                return (run + jnp.sum(_fold8(eq, jnp.sum), axis=0, keepdims=True),
                        jnp.maximum(ibm8, _fold8(taken, jnp.max)))

            _, ibm8 = lax.fori_loop(0, nk, body,
                                    (jnp.zeros((1, T), F32), jnp.zeros((8, T), F32)))
            ibm = jnp.max(ibm8, axis=0, keepdims=True).astype(I32)
            ib_scr[b] = jnp.where(ties[b] > 0.0, ibm, IDX_BIG)

    ibs = [ib_scr[b] for b in NB]

    m_scr[...] = jnp.full(m_scr.shape, NEG_BIG, F32)
    l_scr[...] = jnp.zeros(l_scr.shape, F32)
    acc_scr[...] = jnp.zeros(acc_scr.shape, F32)

    def attn_body(j, carry):
        m_new = {}
        for b in NB:
            t_el = jnp.where((rowk + j * T) < ibs[b], thrs[b], thrs[b] + 1)
            neg = jnp.where(key_scr[b, j] >= t_el, 0.0, NEG_BIG).astype(BF16)
            for h in range(N_HEADS_A):
                sb = r_scr[b, h].astype(BF16) + neg
                mc = jnp.max(_fold16(sb, jnp.max), axis=0, keepdims=True).astype(F32)
                m_new[b, h] = jnp.maximum(m_scr[b, h], mc)
            for h in range(N_HEADS_A):
                p_scr[b, h] = jnp.exp2((r_scr[b, h] - m_new[b, h]).astype(BF16) + neg)
        attn_dots(jnp.minimum(j + 1, i))
        for b in NB:
            vt1 = jnp.concatenate([vat_ref[b, j], jnp.ones((16, T), BF16)], axis=0)
            for h in range(N_HEADS_A):
                alpha = jnp.exp2(m_scr[b, h] - m_new[b, h])
                pv = jnp.dot(vt1, p_scr[b, h], preferred_element_type=F32)
                l_scr[b, h] = alpha * l_scr[b, h] + pv[HEAD_DIM:HEAD_DIM + 1]
                acc_scr[b, h] = alpha * acc_scr[b, h] + pv[0:HEAD_DIM]
                m_scr[b, h] = m_new[b, h]
        return carry

    lax.fori_loop(0, nk, attn_body, 0)
    for b in NB:
        _store_heads(o_ref, g_ref, [acc_scr[b, h] / l_scr[b, h] for h in range(N_HEADS_A)], b)


def _dsa(qi, qa, w, kaki, vat, g, topk):
    b, nq, _, _ = qa.shape
    T = ATT_TILE
    l = nq * T
    nb = _rows_per_step(b)
    return pl.pallas_call(
        functools.partial(_dsa_kernel, tile=T, topk=topk, nb=nb),
        grid=(b // nb, nq),
        in_specs=[
            pl.BlockSpec((nb, 1, 128, 4 * T), lambda bb, i: (bb, i, 0, 0)),
            pl.BlockSpec((nb, 1, 128, 4 * T), lambda bb, i: (bb, i, 0, 0)),
            pl.BlockSpec((nb, 1, 1, 4 * T), lambda bb, i: (bb, i, 0, 0)),
            pl.BlockSpec((nb, l, 128), lambda bb, i: (bb, 0, 0)),
            pl.BlockSpec((nb, nq, 64, T), lambda bb, i: (bb, 0, 0, 0)),
            pl.BlockSpec((1, 256), lambda bb, i: (0, 0)),
        ],
        out_specs=pl.BlockSpec((nb, T, 256), lambda bb, i: (bb, i, 0)),
        out_shape=jax.ShapeDtypeStruct((b, l, 256), BF16),
        scratch_shapes=[
            pltpu.VMEM((nb, nq + nq % 2, T, T), I32),
            pltpu.VMEM((nb, N_HEADS_A, T, T), F32),
            pltpu.VMEM((nb, N_HEADS_A, T, T), F32),
            pltpu.VMEM((nb, N_HEADS_A, T, T), BF16),
            pltpu.VMEM((nb, 1, T), I32),
            pltpu.VMEM((nb, N_HEADS_A, 1, T), F32),
            pltpu.VMEM((nb, N_HEADS_A, 1, T), F32),
            pltpu.VMEM((nb, N_HEADS_A, HEAD_DIM, T), F32),
        ],
        compiler_params=_cparams(("parallel", "arbitrary")),
        name="dsa",
    )(qi, qa, w, kaki, vat, g)


def _sb_kernel(qb_ref, kb_ref, vbt_ref, g_ref, o_ref, run_scr, acc_scr, z_scr, e_scr, c_scr, hl_scr,
               *, tile, nb):
    T = tile
    NB = range(nb)
    i = pl.program_id(1)
    rowk = lax.broadcasted_iota(I32, (T, T), 0)
    colq = lax.broadcasted_iota(I32, (T, T), 1)
    strict = rowk < colq
    tmat = jnp.where(colq > rowk, 1.0, 0.0).astype(BF16)
    run_scr[...] = jnp.zeros(run_scr.shape, F32)
    acc_scr[...] = jnp.zeros(acc_scr.shape, F32)

    def scores(j):
        row0 = pl.multiple_of(j * T, T)
        for b in NB:
            for h in range(N_HEADS_B):
                p = h // 2
                kc = kb_ref[b, pl.ds(row0, T), 128 * p:128 * p + 128]
                z_scr[b, h] = jnp.dot(kc, qb_ref[b, 0, :, h * T:(h + 1) * T],
                                      preferred_element_type=F32)

    def chunk(j, diag):
        tots = {}
        for b in NB:
            for h in range(N_HEADS_B):
                z = z_scr[b, h]
                nz = -z
                lk = jnp.minimum(nz, 0.0) - jnp.log2(1.0 + jnp.exp2(jnp.minimum(z, nz)))
                if diag:
                    lk = jnp.where(strict, lk, 0.0)
                hl_scr[b, h] = lk.astype(BF16)
                e_scr[b, h] = z + lk
                c_scr[b, h] = jnp.dot(tmat, hl_scr[b, h], preferred_element_type=F32)
                tots[b, h] = c_scr[b, h][0:1, :] + lk[0:1, :]
        scores(jnp.maximum(j - 1, 0))
        for b in NB:
            for h in range(N_HEADS_B):
                a = jnp.exp2(e_scr[b, h] + c_scr[b, h] + run_scr[b, h])
                if diag:
                    a = jnp.where(strict, a, 0.0)
                acc_scr[b, h] = acc_scr[b, h] + jnp.dot(vbt_ref[b, j, 64 * h:64 * h + 64, :],
                                                        a.astype(BF16), preferred_element_type=F32)
                run_scr[b, h] = run_scr[b, h] + tots[b, h]

    scores(i)
    chunk(i, True)

    def body(jj, carry):
        chunk(i - 1 - jj, False)
        return carry

    lax.fori_loop(0, i, body, 0)
    for b in NB:
        _store_heads(o_ref, g_ref, [acc_scr[b, h] for h in range(N_HEADS_B)], b)


def _sb(qb, kb, vbt, g):
    b, nq, _, _ = qb.shape
    T = ATT_TILE
    l = nq * T
    nb = _rows_per_step(b)
    return pl.pallas_call(
        functools.partial(_sb_kernel, tile=T, nb=nb),
        grid=(b // nb, nq),
        in_specs=[
            pl.BlockSpec((nb, 1, 128, 4 * T), lambda bb, i: (bb, i, 0, 0)),
            pl.BlockSpec((nb, l, 256), lambda bb, i: (bb, 0, 0)),
            pl.BlockSpec((nb, nq, 256, T), lambda bb, i: (bb, 0, 0, 0)),
            pl.BlockSpec((1, 256), lambda bb, i: (0, 0)),
        ],
        out_specs=pl.BlockSpec((nb, T, 256), lambda bb, i: (bb, i, 0)),
        out_shape=jax.ShapeDtypeStruct((b, l, 256), BF16),
        scratch_shapes=[
            pltpu.VMEM((nb, N_HEADS_B, 1, T), F32),
            pltpu.VMEM((nb, N_HEADS_B, HEAD_DIM, T), F32),
            pltpu.VMEM((nb, N_HEADS_B, T, T), F32),
            pltpu.VMEM((nb, N_HEADS_B, T, T), F32),
            pltpu.VMEM((nb, N_HEADS_B, T, T), F32),
            pltpu.VMEM((nb, N_HEADS_B, T, T), BF16),
        ],
        compiler_params=_cparams(("parallel", "arbitrary")),
        name="stickbreak",
    )(qb, kb, vbt, g)


def _swa_kernel(sink_ref, qc_ref, kc_ref, vct_ref, g_ref, o_ref, s_scr, *, tile, nb):
    T = tile
    NKEY = T + WINDOW
    NB = range(nb)
    i = pl.program_id(1)
    start = pl.multiple_of(jnp.maximum(i * T - WINDOW, 0), WINDOW)
    kidx = start + lax.broadcasted_iota(I32, (NKEY, T), 0)
    tq = i * T + lax.broadcasted_iota(I32, (NKEY, T), 1)
    d = tq - kidx
    band = (d >= 0) & (d < WINDOW)
    c0 = start // WINDOW
    per_kv = N_HEADS_C // N_KV_C
    for b in NB:
        kc = kc_ref[b, pl.ds(start, NKEY), :]
        for h in range(N_HEADS_C):
            s_scr[b, h] = jnp.dot(
                kc, qc_ref[b, 0, h // per_kv, :, (h % per_kv) * T:(h % per_kv + 1) * T],
                preferred_element_type=F32)
    neg = jnp.where(band, 0.0, NEG_BIG).astype(BF16)
    ones = jnp.ones((16, WINDOW), BF16)
    for b in NB:
        outs = []
        for h in range(N_HEADS_C):
            g = h // per_kv
            sink = sink_ref[h] * LOG2E
            sb = s_scr[b, h].astype(BF16) + neg
            m = jnp.maximum(jnp.max(_fold16(sb, jnp.max), axis=0, keepdims=True).astype(F32), sink)
            p = jnp.exp2((s_scr[b, h] - m).astype(BF16) + neg)
            o = None
            for c in range(NKEY // WINDOW):
                vt1 = jnp.concatenate([vct_ref[b, c0 + c, 64 * g:64 * g + 64, :], ones], axis=0)
                part = jnp.dot(vt1, p[c * WINDOW:(c + 1) * WINDOW, :],
                               preferred_element_type=F32)
                o = part if o is None else o + part
            den = o[HEAD_DIM:HEAD_DIM + 1] + jnp.exp2(sink - m)
            outs.append(o[0:HEAD_DIM] / den)
        _store_heads(o_ref, g_ref, outs, b)


def _swa(sinks, qc, kc, vct, g):
    b, nq = qc.shape[:2]
    T = ATT_TILE
    l = nq * T
    nb = _rows_per_step(b)
    return pl.pallas_call(
        functools.partial(_swa_kernel, tile=T, nb=nb),
        grid=(b // nb, nq),
        in_specs=[
            pl.BlockSpec(memory_space=pltpu.SMEM),
            pl.BlockSpec((nb, 1, 2, 128, 4 * T), lambda bb, i: (bb, i, 0, 0, 0)),
            pl.BlockSpec((nb, l, 128), lambda bb, i: (bb, 0, 0)),
            pl.BlockSpec((nb, l // LANES, 128, LANES), lambda bb, i: (bb, 0, 0, 0)),
            pl.BlockSpec((1, 512), lambda bb, i: (0, 0)),
        ],
        out_specs=pl.BlockSpec((nb, T, 512), lambda bb, i: (bb, i, 0)),
        out_shape=jax.ShapeDtypeStruct((b, l, 512), BF16),
        scratch_shapes=[pltpu.VMEM((nb, N_HEADS_C, T + WINDOW, T), F32)],
        compiler_params=_cparams(("parallel", "arbitrary")),
        name="swa",
    )(sinks, qc, kc, vct, g)


def _ffn_chunks(d_ff):
    chunks, c0 = [], 0
    while c0 < d_ff:
        cw = min(1024, d_ff - c0)
        chunks.append((c0, cw))
        c0 += cw
    return chunks


def _merge_ffn_kernel(x_ref, oa_ref, ob_ref, oc_ref, mod_ref, ln_ref, woa_ref, wob_ref, woc_ref,
                      wg_ref, wu_ref, wd_ref, out_ref):
    x = x_ref[0]
    y = (jnp.dot(oa_ref[0], woa_ref[...], preferred_element_type=F32)
         + jnp.dot(ob_ref[0], wob_ref[...], preferred_element_type=F32)
         + jnp.dot(oc_ref[0], woc_ref[...], preferred_element_type=F32))
    x1 = x + mod_ref[0, 2:3, :] * y
    ms = jnp.mean(x1 * x1, axis=-1, keepdims=True)
    h = x1 * lax.rsqrt(ms + NORM_EPS) * ln_ref[...]
    hb = (h * (1.0 + mod_ref[0, 4:5, :]) + mod_ref[0, 3:4, :]).astype(BF16)
    acc = None
    for c0, cw in _ffn_chunks(wg_ref.shape[1]):
        gt = jnp.dot(hb, wg_ref[:, c0:c0 + cw], preferred_element_type=F32)
        up = jnp.dot(hb, wu_ref[:, c0:c0 + cw], preferred_element_type=F32)
        act = (gt * jax.nn.sigmoid(gt) * up).astype(BF16)
        part = jnp.dot(act, wd_ref[c0:c0 + cw, :], preferred_element_type=F32)
        acc = part if acc is None else acc + part
    out_ref[0] = x1 + mod_ref[0, 5:6, :] * acc


def _merge_ffn(x, oa, ob, oc, mod_l, ln, woa, wob, woc, wg, wu, wd):
    b, l, d = x.shape
    tm = TOK_TILE
    dff = wg.shape[1]
    const2 = lambda bb, i: (0, 0)
    once = pl.Buffered(1)
    return pl.pallas_call(
        _merge_ffn_kernel,
        grid=(b, l // tm),
        in_specs=[
            pl.BlockSpec((1, tm, d), lambda bb, i: (bb, i, 0)),
            pl.BlockSpec((1, tm, 256), lambda bb, i: (bb, i, 0)),
            pl.BlockSpec((1, tm, 256), lambda bb, i: (bb, i, 0)),
            pl.BlockSpec((1, tm, 512), lambda bb, i: (bb, i, 0)),
            pl.BlockSpec((1, 6, d), lambda bb, i: (bb, 0, 0)),
            pl.BlockSpec((1, d), const2),
            pl.BlockSpec((256, d), const2, pipeline_mode=once),
            pl.BlockSpec((256, d), const2, pipeline_mode=once),
            pl.BlockSpec((512, d), const2, pipeline_mode=once),
            pl.BlockSpec((d, dff), const2, pipeline_mode=once),
            pl.BlockSpec((d, dff), const2, pipeline_mode=once),
            pl.BlockSpec((dff, d), const2, pipeline_mode=once),
        ],
        out_specs=pl.BlockSpec((1, tm, d), lambda bb, i: (bb, i, 0)),
        out_shape=jax.ShapeDtypeStruct((b, l, d), F32),
        compiler_params=_cparams(("parallel", "arbitrary")),
        name="merge_ffn",
    )(x, oa, ob, oc, mod_l, ln, woa, wob, woc, wg, wu, wd)


def _rope_tables(l):
    inv = 1.0 / (ROPE_THETA ** (jnp.arange(0, HEAD_DIM, 2, dtype=F32) / HEAD_DIM))
    ang = jnp.arange(l, dtype=F32)[:, None] * inv[None, :]
    cos, sin = jnp.cos(ang), jnp.sin(ang)
    cos_n = jnp.tile(cos, (1, 4))
    sin_n = jnp.concatenate([-sin, sin, -sin, sin], axis=1)
    return cos.T, sin.T, cos_n, sin_n


def _split_w_in(w_in):
    widths = (256, 64, 64, 256, 64, 4, 256, 256, 256, 512, 128, 128)
    offs = [0]
    for w in widths:
        offs.append(offs[-1] + w)
    qa, ka, va, qi, ki, wi, qb, kb, vb, qc, kc, vc = [w_in[:, :, offs[k]:offs[k + 1]]
                                                      for k in range(len(widths))]
    idx_scale = float((N_IDX_HEADS * HEAD_DIM) ** -0.5)
    att_scale = float(HEAD_DIM ** -0.5)
    pad = jnp.zeros(wi.shape[:2] + (R_TOTAL - R_W - N_IDX_HEADS,), w_in.dtype)
    wt = jnp.concatenate([qa, qi, qb * (att_scale * LOG2E), qc, va, vb, vc, wi * idx_scale, pad],
                         axis=2)
    wt = jnp.swapaxes(wt, 1, 2).astype(BF16)
    wn = jnp.concatenate([ka, ki, kb, kc], axis=2).astype(BF16)
    return wt, wn


@jax.jit
def kernel(x, c, ln1, ln2, w_mod, b_mod, w_in, qn_a, kn_a, qn_c, kn_c, sinks, g_out, w_o,
           w_gate, w_up, w_down):
    depth = w_in.shape[0]
    b, l, d = x.shape
    assert l % TOK_TILE == 0 and l >= ATT_TILE + WINDOW
    topk = min(TOPK_MAX, l // 4)
    att_scale = float(HEAD_DIM ** -0.5)

    tabs = _rope_tables(l)
    mod = _modulation(c, w_mod, b_mod)
    wt, wn = _split_w_in(w_in)
    w_o_b = w_o.astype(BF16)
    w_g_b, w_u_b, w_d_b = w_gate.astype(BF16), w_up.astype(BF16), w_down.astype(BF16)
    ones = jnp.ones((HEAD_DIM,), F32)

    for li in range(depth):
        gains = ((qn_a[li] * (att_scale * LOG2E)).reshape(HEAD_DIM, 1),
                 (qn_c[li] * (att_scale * LOG2E)).reshape(HEAD_DIM, 1),
                 jnp.concatenate([kn_a[li], ones]).reshape(1, LANES),
                 jnp.concatenate([kn_c[li], kn_c[li]]).reshape(1, LANES))
        (qa, qi, qb, qc, w, vat, vbt, vct, kaki, kb, kc) = _inproj(
            x, mod[li], ln1[li].reshape(1, d), wt[li], wn[li], tabs, gains)
        g = g_out[li].reshape(1, -1)
        oa = _dsa(qi, qa, w, kaki, vat, g[:, 0:256], topk)
        ob = _sb(qb, kb, vbt, g[:, 256:512])
        oc = _swa(sinks[li], qc, kc, vct, g[:, 512:1024])
        x = _merge_ffn(x, oa, ob, oc, mod[li], ln2[li].reshape(1, d),
                       w_o_b[li, 0:256], w_o_b[li, 256:512], w_o_b[li, 512:1024],
                       w_g_b[li], w_u_b[li], w_d_b[li])
    return x
```

```python
import functools

import jax
import jax.numpy as jnp
from jax import lax
from jax.experimental import pallas as pl
from jax.experimental.pallas import tpu as pltpu

F32 = jnp.float32
BF16 = jnp.bfloat16
I32 = jnp.int32

HEAD_DIM = 64
N_HEADS_A = 4
N_HEADS_B = 4
N_HEADS_C = 8
N_KV_C = 2
N_IDX_HEADS = 4
TOPK_MAX = 256
WINDOW = 128
ROPE_THETA = 10000.0
NORM_EPS = 1e-6
LANES = 128
ATT_TILE = 256
TOK_TILE = 512
INPROJ_TILE = 1024
VMEM_LIMIT = 48 * 1024 * 1024

NEG_BIG = -1e30
INT_MIN = -2 ** 31
INT_MAX = 2 ** 31 - 1
KEY_NEG_INF = -0x7F800000
IDX_BIG = 2 ** 30
LOG2E = 1.4426950408889634
FIRST_PROBES = 14
PROBES_PER_CHECK = 2
FLOAT_PROBES = 30

R_QA, R_QI, R_QB, R_QC = 0, 256, 512, 768
R_VA, R_VB, R_VC, R_W = 1280, 1344, 1600, 1728
R_TOTAL = 1744
C_KAKI, C_KB, C_KC = 0, 128, 384
C_TOTAL = 512


def _cparams(sem):
    return pltpu.CompilerParams(dimension_semantics=sem, vmem_limit_bytes=VMEM_LIMIT)


def _rows_per_step(b):
    return 4 if b % 4 == 0 else (2 if b % 2 == 0 else 1)


def _mod_kernel(c_ref, w_ref, b_ref, o_ref):
    c = c_ref[...]
    ca = (c * jax.nn.sigmoid(c)).astype(BF16)
    y = jnp.dot(ca, w_ref[0].astype(BF16), preferred_element_type=F32)
    o_ref[0, 0] = y + b_ref[0, 0]


def _modulation(c, w_mod, b_mod):
    depth, d, _ = w_mod.shape
    b = c.shape[0]
    out = pl.pallas_call(
        _mod_kernel,
        grid=(depth, 6),
        in_specs=[
            pl.BlockSpec((b, d), lambda l, k: (0, 0)),
            pl.BlockSpec((1, d, d), lambda l, k: (l, 0, k)),
            pl.BlockSpec((1, 1, 1, d), lambda l, k: (l, k, 0, 0)),
        ],
        out_specs=pl.BlockSpec((1, 1, b, d), lambda l, k: (l, k, 0, 0)),
        out_shape=jax.ShapeDtypeStruct((depth, 6, b, d), F32),
        compiler_params=_cparams(("arbitrary", "arbitrary")),
        name="modulation",
    )(c, w_mod, b_mod.reshape(depth, 6, 1, d))
    return jnp.transpose(out, (0, 2, 1, 3))


def _inproj_kernel(x_ref, mod_ref, ln_ref, wt_ref, wn_ref, cost_ref, sint_ref, cosn_ref, sinn_ref,
                   gqa_ref, gqc_ref, gka_ref, gkc_ref,
                   qa_ref, qi_ref, qb_ref, qc_ref, w_ref, vat_ref, vbt_ref, vct_ref,
                   kaki_ref, kb_ref, kc_ref, *, tile):
    T = tile
    nc = x_ref.shape[1] // T
    gqa = gqa_ref[...]
    gqc = gqc_ref[...]
    lane = lax.broadcasted_iota(I32, (T, LANES), 1)
    left = lane < HEAD_DIM
    first = (lane & 32) == 0
    zero_half = jnp.zeros((HEAD_DIM, T), BF16)
    per_kv = N_HEADS_C // N_KV_C

    for c in range(nc):
        tok = slice(c * T, (c + 1) * T)
        x = x_ref[0, tok, :]
        ms = jnp.mean(x * x, axis=-1, keepdims=True)
        h = x * lax.rsqrt(ms + NORM_EPS) * ln_ref[...]
        h = h * (1.0 + mod_ref[0, 1:2, :]) + mod_ref[0, 0:1, :]
        hb = h.astype(BF16)
        pt = lax.dot_general(wt_ref[...], hb, (((1,), (1,)), ((), ())),
                             preferred_element_type=F32)
        pn = jnp.dot(hb, wn_ref[...], preferred_element_type=F32)

        cos_t = cost_ref[:, tok]
        sin_t = sint_ref[:, tok]

        def rope_t(y):
            y1, y2 = y[0:32], y[32:64]
            return jnp.concatenate([y1 * cos_t - y2 * sin_t, y2 * cos_t + y1 * sin_t], axis=0)

        def rms_t(y, g):
            m = jnp.mean(y * y, axis=0, keepdims=True)
            return y * lax.rsqrt(m + NORM_EPS) * g

        def put_padded(ref, lead, y, half, col):
            idx = (0, c) + lead
            ref[idx + (slice(64 * half, 64 * half + 64), slice(col * T, (col + 1) * T))] = \
                y.astype(BF16)
            ref[idx + (slice(64 * (1 - half), 64 * (1 - half) + 64),
                       slice(col * T, (col + 1) * T))] = zero_half

        for hh in range(N_HEADS_A):
            put_padded(qa_ref, (), rope_t(rms_t(pt[R_QA + 64 * hh:R_QA + 64 * hh + 64], gqa)), 0, hh)
        for hh in range(N_IDX_HEADS):
            put_padded(qi_ref, (), rope_t(pt[R_QI + 64 * hh:R_QI + 64 * hh + 64]), 1, hh)
        for hh in range(N_HEADS_B):
            put_padded(qb_ref, (), pt[R_QB + 64 * hh:R_QB + 64 * hh + 64], hh % 2, hh)
        for hh in range(N_HEADS_C):
            g = hh // per_kv
            put_padded(qc_ref, (g,), rope_t(rms_t(pt[R_QC + 64 * hh:R_QC + 64 * hh + 64], gqc)),
                       g, hh % per_kv)
        vat_ref[0, c] = pt[R_VA:R_VA + 64].astype(BF16)
        vbt_ref[0, c] = pt[R_VB:R_VB + 256].astype(BF16)
        for hh in range(N_IDX_HEADS):
            w_ref[0, c, :, hh * T:(hh + 1) * T] = pt[R_W + hh:R_W + hh + 1]
        for cc in range(T // LANES):
            vct_ref[0, c * (T // LANES) + cc] = \
                pt[R_VC:R_VC + 128, cc * LANES:(cc + 1) * LANES].astype(BF16)

        cos_n = cosn_ref[tok, :]
        sin_n = sinn_ref[tok, :]

        def rope_n(y):
            rot = jnp.where(first, pltpu.roll(y, 96, 1), pltpu.roll(y, 32, 1))
            return y * cos_n + rot * sin_n

        def head_rsqrt(y):
            sq = y * y
            s0 = jnp.sum(jnp.where(left, sq, 0.0), axis=-1, keepdims=True)
            s1 = jnp.sum(jnp.where(left, 0.0, sq), axis=-1, keepdims=True)
            return lax.rsqrt(jnp.where(left, s0, s1) * (1.0 / HEAD_DIM) + NORM_EPS)

        y = pn[:, C_KAKI:C_KAKI + 128]
        fac = jnp.where(left, head_rsqrt(y), 1.0) * gka_ref[...]
        kaki_ref[0, tok, :] = rope_n(y * fac).astype(BF16)
        kb_ref[0, tok, :] = pn[:, C_KB:C_KB + 256].astype(BF16)
        y = pn[:, C_KC:C_KC + 128]
        kc_ref[0, tok, :] = rope_n(y * head_rsqrt(y) * gkc_ref[...]).astype(BF16)


def _inproj(x, mod_l, ln, wt, wn, tabs, gains):
    b, l, d = x.shape
    T = ATT_TILE
    tm = INPROJ_TILE if l % INPROJ_TILE == 0 else TOK_TILE
    nq = l // T
    nc = tm // T
    cos_t, sin_t, cos_n, sin_n = tabs
    gqa, gqc, gka, gkc = gains
    const2 = lambda bb, i: (0, 0)
    out_shape = (
        jax.ShapeDtypeStruct((b, nq, 128, 4 * T), BF16),
        jax.ShapeDtypeStruct((b, nq, 128, 4 * T), BF16),
        jax.ShapeDtypeStruct((b, nq, 128, 4 * T), BF16),
        jax.ShapeDtypeStruct((b, nq, 2, 128, 4 * T), BF16),
        jax.ShapeDtypeStruct((b, nq, 1, 4 * T), F32),
        jax.ShapeDtypeStruct((b, nq, 64, T), BF16),
        jax.ShapeDtypeStruct((b, nq, 256, T), BF16),
        jax.ShapeDtypeStruct((b, l // LANES, 128, LANES), BF16),
        jax.ShapeDtypeStruct((b, l, 128), BF16),
        jax.ShapeDtypeStruct((b, l, 256), BF16),
        jax.ShapeDtypeStruct((b, l, 128), BF16),
    )
    out_specs = (
        pl.BlockSpec((1, nc, 128, 4 * T), lambda bb, i: (bb, i, 0, 0)),
        pl.BlockSpec((1, nc, 128, 4 * T), lambda bb, i: (bb, i, 0, 0)),
        pl.BlockSpec((1, nc, 128, 4 * T), lambda bb, i: (bb, i, 0, 0)),
        pl.BlockSpec((1, nc, 2, 128, 4 * T), lambda bb, i: (bb, i, 0, 0, 0)),
        pl.BlockSpec((1, nc, 1, 4 * T), lambda bb, i: (bb, i, 0, 0)),
        pl.BlockSpec((1, nc, 64, T), lambda bb, i: (bb, i, 0, 0)),
        pl.BlockSpec((1, nc, 256, T), lambda bb, i: (bb, i, 0, 0)),
        pl.BlockSpec((1, tm // LANES, 128, LANES), lambda bb, i: (bb, i, 0, 0)),
        pl.BlockSpec((1, tm, 128), lambda bb, i: (bb, i, 0)),
        pl.BlockSpec((1, tm, 256), lambda bb, i: (bb, i, 0)),
        pl.BlockSpec((1, tm, 128), lambda bb, i: (bb, i, 0)),
    )
    in_specs = [
        pl.BlockSpec((1, tm, d), lambda bb, i: (bb, i, 0)),
        pl.BlockSpec((1, 6, d), lambda bb, i: (bb, 0, 0)),
        pl.BlockSpec((1, d), const2),
        pl.BlockSpec((R_TOTAL, d), const2),
        pl.BlockSpec((d, C_TOTAL), const2),
        pl.BlockSpec((32, tm), lambda bb, i: (0, i)),
        pl.BlockSpec((32, tm), lambda bb, i: (0, i)),
        pl.BlockSpec((tm, LANES), lambda bb, i: (i, 0)),
        pl.BlockSpec((tm, LANES), lambda bb, i: (i, 0)),
        pl.BlockSpec((HEAD_DIM, 1), const2),
        pl.BlockSpec((HEAD_DIM, 1), const2),
        pl.BlockSpec((1, LANES), const2),
        pl.BlockSpec((1, LANES), const2),
    ]
    return pl.pallas_call(
        functools.partial(_inproj_kernel, tile=T),
        grid=(b, l // tm),
        in_specs=in_specs,
        out_specs=out_specs,
        out_shape=out_shape,
        compiler_params=_cparams(("parallel", "arbitrary")),
        name="inproj",
    )(x, mod_l, ln, wt, wn, cos_t, sin_t, cos_n, sin_n, gqa, gqc, gka, gkc)


def _store_heads(o_ref, g_ref, heads_t, row):
    for p in range(len(heads_t) // 2):
        parts = []
        for o in heads_t[2 * p:2 * p + 2]:
            ms = jnp.mean(o * o, axis=0, keepdims=True)
            parts.append(o * lax.rsqrt(ms + NORM_EPS))
        ot = jnp.concatenate(parts, axis=0).T
        o_ref[row, :, 128 * p:128 * p + 128] = (ot * g_ref[:, 128 * p:128 * p + 128]).astype(BF16)


def _float_key(x):
    kb = pltpu.bitcast(x, I32)
    return jnp.where(kb < 0, INT_MIN - kb, kb)


def _key_float(k):
    return pltpu.bitcast(jnp.where(k < 0, INT_MIN - k, k), F32)


def _fold16(x, op):
    t = x.shape[0]
    return op(x.reshape(t // 16, 16, x.shape[1]), axis=0)


def _fold8(x, op):
    t = x.shape[0]
    return op(x.reshape(t // 8, 8, x.shape[1]), axis=0)


def _dsa_kernel(qi_ref, qa_ref, w_ref, kaki_ref, vat_ref, g_ref, o_ref,
                key_scr, s_scr, r_scr, p_scr, ib_scr, m_scr, l_scr, acc_scr, *, tile, topk, nb):
    T = tile
    K = topk
    NB = range(nb)
    i = pl.program_id(1)
    nk = i + 1
    rowk = lax.broadcasted_iota(I32, (T, T), 0)
    colq = lax.broadcasted_iota(I32, (T, T), 1)
    negk = -rowk

    def key_chunk(b, j):
        return kaki_ref[b, pl.ds(pl.multiple_of(j * T, T), T), :]

    def index_dots(j):
        for b in NB:
            kc = key_chunk(b, j)
            for h in range(N_IDX_HEADS):
                s_scr[b, h] = jnp.dot(kc, qi_ref[b, 0, :, h * T:(h + 1) * T],
                                      preferred_element_type=F32)

    def attn_dots(j):
        for b in NB:
            kc = key_chunk(b, j)
            for h in range(N_HEADS_A):
                r_scr[b, h] = jnp.dot(kc, qa_ref[b, 0, :, h * T:(h + 1) * T],
                                      preferred_element_type=F32)

    def score_chunk(j, nxt, diag, mm):
        scs = []
        for b in NB:
            sc = None
            for h in range(N_IDX_HEADS):
                r = jnp.maximum(s_scr[b, h], 0.0) * w_ref[b, 0, :, h * T:(h + 1) * T]
                sc = r if sc is None else sc + r
            scs.append(sc)
        index_dots(nxt)
        out = []
        for b in NB:
            sc = scs[b]
            key = jnp.where(sc == 0.0, negk - j * T, _float_key(sc))
            if diag:
                causal = rowk <= colq
                key_hi = jnp.where(causal, key, KEY_NEG_INF)
                key_lo = jnp.where(causal, key, INT_MAX)
            else:
                key_hi = key_lo = key
            key_scr[b, j] = key_hi
            mx8, mn8 = mm[b]
            out.append((jnp.maximum(mx8, _fold8(key_hi, jnp.max)),
                        jnp.minimum(mn8, _fold8(key_lo, jnp.min))))
        return tuple(out)

    index_dots(i)
    attn_dots(0)
    mm0 = tuple((jnp.full((8, T), INT_MIN, I32), jnp.full((8, T), INT_MAX, I32)) for _ in NB)
    mm = score_chunk(i, 0, True, mm0)
    mm = lax.fori_loop(0, i, lambda j, c: score_chunk(j, jnp.minimum(j + 1, i - 1), False, c), mm)

    def count_ge(mids):
        def body(p, accs):
            out = []
            for b in NB:
                a = _fold8(jnp.where(key_scr[b, 2 * p] >= mids[b], 1, 0), jnp.sum)
                c = _fold8(jnp.where(key_scr[b, 2 * p + 1] >= mids[b], 1, 0), jnp.sum)
                out.append(accs[b] + (a + c))
            return tuple(out)
        accs = lax.fori_loop(0, (nk + 1) // 2, body, tuple(jnp.zeros((8, T), I32) for _ in NB))
        return [jnp.sum(a, axis=0, keepdims=True) for a in accs]

    @pl.when(nk % 2 == 1)
    def _():
        for b in NB:
            key_scr[b, nk] = jnp.full((T, T), INT_MIN, I32)

    tq = i * T + lax.broadcasted_iota(I32, (1, T), 1)
    big = tq >= K
    zero_lo = -nk * T

    def pick(it, lo, hi, clo, chi):
        f = ((clo - K).astype(F32) + 0.5) / jnp.maximum(clo - chi, 1).astype(F32)
        frac = jnp.where(f < 0.25, jnp.maximum(2.0 * f, 1e-3),
                         jnp.where(f > 0.75, 1.0 - jnp.maximum(2.0 - 2.0 * f, 1e-3), 0.5))
        fmid = _float_key(_key_float(lo) * (1.0 - frac) + _key_float(hi) * frac)
        kmid = (lo >> 1) + (hi >> 1) + (lo & hi & 1)
        inside = jnp.where(fmid > lo, jnp.where(fmid < hi, it, FLOAT_PROBES), FLOAT_PROBES)
        mid = jnp.where(inside < FLOAT_PROBES, fmid, kmid)
        mid = jnp.where(lo == 0, jnp.where(hi > 1, 1, mid), mid)
        mid = jnp.where(lo < zero_lo, jnp.where(hi > zero_lo, zero_lo, mid), mid)
        return jnp.where(lo < 0, jnp.where(hi > 0, 0, mid), mid)

    def probes(n, st):
        _, it, brs = st
        for _ in range(n):
            mids = [pick(it, *brs[b]) for b in NB]
            cs = count_ge(mids)
            new = []
            for b in NB:
                lo, hi, clo, chi = brs[b]
                ge = cs[b] >= K
                new.append((jnp.where(ge, mids[b], lo), jnp.where(ge, hi, mids[b]),
                            jnp.where(ge, cs[b], clo), jnp.where(ge, chi, cs[b])))
            brs = tuple(new)
            it = it + 1
        nopen = 0.0
        for b in NB:
            lo, hi, clo, chi = brs[b]
            nopen = nopen + jnp.sum(jnp.where(clo == K, 0.0, jnp.where((hi - 1) == lo, 0.0, 1.0)))
        return nopen, it, brs

    brs0 = tuple((jnp.where(big, jnp.min(mm[b][1], axis=0, keepdims=True), KEY_NEG_INF),
                  jnp.where(big, jnp.max(mm[b][0], axis=0, keepdims=True) + 1, KEY_NEG_INF + 1),
                  tq + 1, jnp.zeros((1, T), I32)) for b in NB)
    st = probes(FIRST_PROBES, (0.0, jnp.zeros((1, T), I32), brs0))
    _, _, brs = lax.while_loop(lambda st: st[0] > 0.0,
                               functools.partial(probes, PROBES_PER_CHECK), st)

    ties = [jnp.where(big, jnp.where(brs[b][2] > K, 1.0, 0.0), 0.0) for b in NB]
    thrs = [jnp.maximum(brs[b][0], KEY_NEG_INF + 1) for b in NB]
    ntie = 0.0
    for b in NB:
        ib_scr[b] = jnp.full((1, T), IDX_BIG, I32)
        ntie = ntie + jnp.sum(ties[b])

    @pl.when(ntie > 0.0)
    def _():
        before = jnp.where(colq < rowk, 1.0, 0.0).astype(BF16)
        for b in NB:
            need = (K - brs[b][3]).astype(F32)

            def body(j, carry, b=b, need=need):
                run, ibm8 = carry
                eq = jnp.where(key_scr[b, j] == thrs[b], 1.0, 0.0)
                pc = jnp.dot(before, eq.astype(BF16), preferred_element_type=F32) + run
                idx1 = (rowk + (j * T + 1)).astype(F32)
                taken = jnp.where(pc < need, eq * idx1, 0.0)
                return (run + jnp.sum(_fold8(eq, jnp.sum), axis=0, keepdims=True),
                        jnp.maximum(ibm8, _fold8(taken, jnp.max)))

            _, ibm8 = lax.fori_loop(0, nk, body,
                                    (jnp.zeros((1, T), F32), jnp.zeros((8, T), F32)))
            ibm = jnp.max(ibm8, axis=0, keepdims=True).astype(I32)
            ib_scr[b] = jnp.where(ties[b] > 0.0, ibm, IDX_BIG)

    ibs = [ib_scr[b] for b in NB]

    m_scr[...] = jnp.full(m_scr.shape, NEG_BIG, F32)
    l_scr[...] = jnp.zeros(l_scr.shape, F32)
    acc_scr[...] = jnp.zeros(acc_scr.shape, F32)

    def attn_body(j, carry):
        m_new = {}
        for b in NB:
            t_el = jnp.where((rowk + j * T) < ibs[b], thrs[b], thrs[b] + 1)
            neg = jnp.where(key_scr[b, j] >= t_el, 0.0, NEG_BIG).astype(BF16)
            for h in range(N_HEADS_A):
                sb = r_scr[b, h].astype(BF16) + neg
                mc = jnp.max(_fold16(sb, jnp.max), axis=0, keepdims=True).astype(F32)
                m_new[b, h] = jnp.maximum(m_scr[b, h], mc)
            for h in range(N_HEADS_A):
                p_scr[b, h] = jnp.exp2((r_scr[b, h] - m_new[b, h]).astype(BF16) + neg)
        attn_dots(jnp.minimum(j + 1, i))
        for b in NB:
            vt1 = jnp.concatenate([vat_ref[b, j], jnp.ones((16, T), BF16)], axis=0)
            for h in range(N_HEADS_A):
                alpha = jnp.exp2(m_scr[b, h] - m_new[b, h])
                pv = jnp.dot(vt1, p_scr[b, h], preferred_element_type=F32)
                l_scr[b, h] = alpha * l_scr[b, h] + pv[HEAD_DIM:HEAD_DIM + 1]
                acc_scr[b, h] = alpha * acc_scr[b, h] + pv[0:HEAD_DIM]
                m_scr[b, h] = m_new[b, h]
        return carry

    lax.fori_loop(0, nk, attn_body, 0)
    for b in NB:
        _store_heads(o_ref, g_ref, [acc_scr[b, h] / l_scr[b, h] for h in range(N_HEADS_A)], b)


def _dsa(qi, qa, w, kaki, vat, g, topk):
    b, nq, _, _ = qa.shape
    T = ATT_TILE
    l = nq * T
    nb = _rows_per_step(b)
    return pl.pallas_call(
        functools.partial(_dsa_kernel, tile=T, topk=topk, nb=nb),
        grid=(b // nb, nq),
        in_specs=[
            pl.BlockSpec((nb, 1, 128, 4 * T), lambda bb, i: (bb, i, 0, 0)),
            pl.BlockSpec((nb, 1, 128, 4 * T), lambda bb, i: (bb, i, 0, 0)),
            pl.BlockSpec((nb, 1, 1, 4 * T), lambda bb, i: (bb, i, 0, 0)),
            pl.BlockSpec((nb, l, 128), lambda bb, i: (bb, 0, 0)),
            pl.BlockSpec((nb, nq, 64, T), lambda bb, i: (bb, 0, 0, 0)),
            pl.BlockSpec((1, 256), lambda bb, i: (0, 0)),
        ],
        out_specs=pl.BlockSpec((nb, T, 256), lambda bb, i: (bb, i, 0)),
        out_shape=jax.ShapeDtypeStruct((b, l, 256), BF16),
        scratch_shapes=[
            pltpu.VMEM((nb, nq + nq % 2, T, T), I32),
            pltpu.VMEM((nb, N_HEADS_A, T, T), F32),
            pltpu.VMEM((nb, N_HEADS_A, T, T), F32),
            pltpu.VMEM((nb, N_HEADS_A, T, T), BF16),
            pltpu.VMEM((nb, 1, T), I32),
            pltpu.VMEM((nb, N_HEADS_A, 1, T), F32),
            pltpu.VMEM((nb, N_HEADS_A, 1, T), F32),
            pltpu.VMEM((nb, N_HEADS_A, HEAD_DIM, T), F32),
        ],
        compiler_params=_cparams(("parallel", "arbitrary")),
        name="dsa",
    )(qi, qa, w, kaki, vat, g)


def _sb_kernel(qb_ref, kb_ref, vbt_ref, g_ref, o_ref, run_scr, acc_scr, z_scr, e_scr, c_scr, hl_scr,
               *, tile, nb):
    T = tile
    NB = range(nb)
    i = pl.program_id(1)
    rowk = lax.broadcasted_iota(I32, (T, T), 0)
    colq = lax.broadcasted_iota(I32, (T, T), 1)
    strict = rowk < colq
    tmat = jnp.where(colq > rowk, 1.0, 0.0).astype(BF16)
    run_scr[...] = jnp.zeros(run_scr.shape, F32)
    acc_scr[...] = jnp.zeros(acc_scr.shape, F32)

    def scores(j):
        row0 = pl.multiple_of(j * T, T)
        for b in NB:
            for h in range(N_HEADS_B):
                p = h // 2
                kc = kb_ref[b, pl.ds(row0, T), 128 * p:128 * p + 128]
                z_scr[b, h] = jnp.dot(kc, qb_ref[b, 0, :, h * T:(h + 1) * T],
                                      preferred_element_type=F32)

    def chunk(j, diag):
        tots = {}
        for b in NB:
            for h in range(N_HEADS_B):
                z = z_scr[b, h]
                nz = -z
                lk = jnp.minimum(nz, 0.0) - jnp.log2(1.0 + jnp.exp2(jnp.minimum(z, nz)))
                if diag:
                    lk = jnp.where(strict, lk, 0.0)
                hl_scr[b, h] = lk.astype(BF16)
                e_scr[b, h] = z + lk
                c_scr[b, h] = jnp.dot(tmat, hl_scr[b, h], preferred_element_type=F32)
                tots[b, h] = c_scr[b, h][0:1, :] + lk[0:1, :]
        scores(jnp.maximum(j - 1, 0))
        for b in NB:
            for h in range(N_HEADS_B):
                a = jnp.exp2(e_scr[b, h] + c_scr[b, h] + run_scr[b, h])
                if diag:
                    a = jnp.where(strict, a, 0.0)
                acc_scr[b, h] = acc_scr[b, h] + jnp.dot(vbt_ref[b, j, 64 * h:64 * h + 64, :],
                                                        a.astype(BF16), preferred_element_type=F32)
                run_scr[b, h] = run_scr[b, h] + tots[b, h]

    scores(i)
    chunk(i, True)

    def body(jj, carry):
        chunk(i - 1 - jj, False)
        return carry

    lax.fori_loop(0, i, body, 0)
    for b in NB:
        _store_heads(o_ref, g_ref, [acc_scr[b, h] for h in range(N_HEADS_B)], b)


def _sb(qb, kb, vbt, g):
    b, nq, _, _ = qb.shape
    T = ATT_TILE
    l = nq * T
    nb = _rows_per_step(b)
    return pl.pallas_call(
        functools.partial(_sb_kernel, tile=T, nb=nb),
        grid=(b // nb, nq),
        in_specs=[
            pl.BlockSpec((nb, 1, 128, 4 * T), lambda bb, i: (bb, i, 0, 0)),
            pl.BlockSpec((nb, l, 256), lambda bb, i: (bb, 0, 0)),
            pl.BlockSpec((nb, nq, 256, T), lambda bb, i: (bb, 0, 0, 0)),
            pl.BlockSpec((1, 256), lambda bb, i: (0, 0)),
        ],
        out_specs=pl.BlockSpec((nb, T, 256), lambda bb, i: (bb, i, 0)),
        out_shape=jax.ShapeDtypeStruct((b, l, 256), BF16),
        scratch_shapes=[
            pltpu.VMEM((nb, N_HEADS_B, 1, T), F32),
            pltpu.VMEM((nb, N_HEADS_B, HEAD_DIM, T), F32),
            pltpu.VMEM((nb, N_HEADS_B, T, T), F32),
            pltpu.VMEM((nb, N_HEADS_B, T, T), F32),
            pltpu.VMEM((nb, N_HEADS_B, T, T), F32),
            pltpu.VMEM((nb, N_HEADS_B, T, T), BF16),
        ],
        compiler_params=_cparams(("parallel", "arbitrary")),
        name="stickbreak",
    )(qb, kb, vbt, g)


def _swa_kernel(sink_ref, qc_ref, kc_ref, vct_ref, g_ref, o_ref, s_scr, *, tile, nb):
    T = tile
    NKEY = T + WINDOW
    NB = range(nb)
    i = pl.program_id(1)
    start = pl.multiple_of(jnp.maximum(i * T - WINDOW, 0), WINDOW)
    kidx = start + lax.broadcasted_iota(I32, (NKEY, T), 0)
    tq = i * T + lax.broadcasted_iota(I32, (NKEY, T), 1)
    d = tq - kidx
    band = (d >= 0) & (d < WINDOW)
    c0 = start // WINDOW
    per_kv = N_HEADS_C // N_KV_C
    for b in NB:
        kc = kc_ref[b, pl.ds(start, NKEY), :]
        for h in range(N_HEADS_C):
            s_scr[b, h] = jnp.dot(
                kc, qc_ref[b, 0, h // per_kv, :, (h % per_kv) * T:(h % per_kv + 1) * T],
                preferred_element_type=F32)
    neg = jnp.where(band, 0.0, NEG_BIG).astype(BF16)
    ones = jnp.ones((16, WINDOW), BF16)
    for b in NB:
        outs = []
        for h in range(N_HEADS_C):
            g = h // per_kv
            sink = sink_ref[h] * LOG2E
            sb = s_scr[b, h].astype(BF16) + neg
            m = jnp.maximum(jnp.max(_fold16(sb, jnp.max), axis=0, keepdims=True).astype(F32), sink)
            p = jnp.exp2((s_scr[b, h] - m).astype(BF16) + neg)
            o = None
            for c in range(NKEY // WINDOW):
                vt1 = jnp.concatenate([vct_ref[b, c0 + c, 64 * g:64 * g + 64, :], ones], axis=0)
                part = jnp.dot(vt1, p[c * WINDOW:(c + 1) * WINDOW, :],
                               preferred_element_type=F32)
                o = part if o is None else o + part
            den = o[HEAD_DIM:HEAD_DIM + 1] + jnp.exp2(sink - m)
            outs.append(o[0:HEAD_DIM] / den)
        _store_heads(o_ref, g_ref, outs, b)


def _swa(sinks, qc, kc, vct, g):
    b, nq = qc.shape[:2]
    T = ATT_TILE
    l = nq * T
    nb = _rows_per_step(b)
    return pl.pallas_call(
        functools.partial(_swa_kernel, tile=T, nb=nb),
        grid=(b // nb, nq),
        in_specs=[
            pl.BlockSpec(memory_space=pltpu.SMEM),
            pl.BlockSpec((nb, 1, 2, 128, 4 * T), lambda bb, i: (bb, i, 0, 0, 0)),
            pl.BlockSpec((nb, l, 128), lambda bb, i: (bb, 0, 0)),
            pl.BlockSpec((nb, l // LANES, 128, LANES), lambda bb, i: (bb, 0, 0, 0)),
            pl.BlockSpec((1, 512), lambda bb, i: (0, 0)),
        ],
        out_specs=pl.BlockSpec((nb, T, 512), lambda bb, i: (bb, i, 0)),
        out_shape=jax.ShapeDtypeStruct((b, l, 512), BF16),
        scratch_shapes=[pltpu.VMEM((nb, N_HEADS_C, T + WINDOW, T), F32)],
        compiler_params=_cparams(("parallel", "arbitrary")),
        name="swa",
    )(sinks, qc, kc, vct, g)


def _sb_swa_kernel(sink_ref, qb_ref, kb_ref, vbt_ref, gb_ref, qc_ref, kc_ref, vct_ref, gc_ref,
                   ob_ref, oc_ref, run_scr, acc_scr, z_scr, e_scr, c_scr, hl_scr, s_scr, *, tile, nb):
    _swa_kernel(sink_ref, qc_ref, kc_ref, vct_ref, gc_ref, oc_ref, s_scr, tile=tile, nb=nb)
    _sb_kernel(qb_ref, kb_ref, vbt_ref, gb_ref, ob_ref, run_scr, acc_scr, z_scr, e_scr, c_scr, hl_scr,
               tile=tile, nb=nb)


def _sb_swa(sinks, qb, kb, vbt, gb, qc, kc, vct, gc):
    b, nq, _, _ = qb.shape
    T = ATT_TILE
    l = nq * T
    nb = 2 if b % 2 == 0 else 1
    return pl.pallas_call(
        functools.partial(_sb_swa_kernel, tile=T, nb=nb),
        grid=(b // nb, nq),
        in_specs=[
            pl.BlockSpec(memory_space=pltpu.SMEM),
            pl.BlockSpec((nb, 1, 128, 4 * T), lambda bb, i: (bb, i, 0, 0)),
            pl.BlockSpec((nb, l, 256), lambda bb, i: (bb, 0, 0)),
            pl.BlockSpec((nb, nq, 256, T), lambda bb, i: (bb, 0, 0, 0)),
            pl.BlockSpec((1, 256), lambda bb, i: (0, 0)),
            pl.BlockSpec((nb, 1, 2, 128, 4 * T), lambda bb, i: (bb, i, 0, 0, 0)),
            pl.BlockSpec((nb, l, 128), lambda bb, i: (bb, 0, 0)),
            pl.BlockSpec((nb, l // LANES, 128, LANES), lambda bb, i: (bb, 0, 0, 0)),
            pl.BlockSpec((1, 512), lambda bb, i: (0, 0)),
        ],
        out_specs=(pl.BlockSpec((nb, T, 256), lambda bb, i: (bb, i, 0)),
                   pl.BlockSpec((nb, T, 512), lambda bb, i: (bb, i, 0))),
        out_shape=(jax.ShapeDtypeStruct((b, l, 256), BF16), jax.ShapeDtypeStruct((b, l, 512), BF16)),
        scratch_shapes=[
            pltpu.VMEM((nb, N_HEADS_B, 1, T), F32),
            pltpu.VMEM((nb, N_HEADS_B, HEAD_DIM, T), F32),
            pltpu.VMEM((nb, N_HEADS_B, T, T), F32),
            pltpu.VMEM((nb, N_HEADS_B, T, T), F32),
            pltpu.VMEM((nb, N_HEADS_B, T, T), F32),
            pltpu.VMEM((nb, N_HEADS_B, T, T), BF16),
            pltpu.VMEM((nb, N_HEADS_C, T + WINDOW, T), F32),
        ],
        compiler_params=_cparams(("parallel", "arbitrary")),
        name="sb_swa",
    )(sinks, qb, kb, vbt, gb, qc, kc, vct, gc)


def _ffn_chunks(d_ff):
    chunks, c0 = [], 0
    while c0 < d_ff:
        cw = min(1024, d_ff - c0)
        chunks.append((c0, cw))
        c0 += cw
    return chunks


def _merge_ffn_kernel(x_ref, oa_ref, ob_ref, oc_ref, mod_ref, ln_ref, woa_ref, wob_ref, woc_ref,
                      wg_ref, wu_ref, wd_ref, out_ref):
    x = x_ref[0]
    y = (jnp.dot(oa_ref[0], woa_ref[...], preferred_element_type=F32)
         + jnp.dot(ob_ref[0], wob_ref[...], preferred_element_type=F32)
         + jnp.dot(oc_ref[0], woc_ref[...], preferred_element_type=F32))
    x1 = x + mod_ref[0, 2:3, :] * y
    ms = jnp.mean(x1 * x1, axis=-1, keepdims=True)
    h = x1 * lax.rsqrt(ms + NORM_EPS) * ln_ref[...]
    hb = (h * (1.0 + mod_ref[0, 4:5, :]) + mod_ref[0, 3:4, :]).astype(BF16)
    acc = None
    for c0, cw in _ffn_chunks(wg_ref.shape[1]):
        gt = jnp.dot(hb, wg_ref[:, c0:c0 + cw], preferred_element_type=F32)
        up = jnp.dot(hb, wu_ref[:, c0:c0 + cw], preferred_element_type=F32)
        act = (gt * jax.nn.sigmoid(gt) * up).astype(BF16)
        part = jnp.dot(act, wd_ref[c0:c0 + cw, :], preferred_element_type=F32)
        acc = part if acc is None else acc + part
    out_ref[0] = x1 + mod_ref[0, 5:6, :] * acc


def _merge_ffn(x, oa, ob, oc, mod_l, ln, woa, wob, woc, wg, wu, wd):
    b, l, d = x.shape
    tm = TOK_TILE
    dff = wg.shape[1]
    const2 = lambda bb, i: (0, 0)
    once = pl.Buffered(1)
    return pl.pallas_call(
        _merge_ffn_kernel,
        grid=(b, l // tm),
        in_specs=[
            pl.BlockSpec((1, tm, d), lambda bb, i: (bb, i, 0)),
            pl.BlockSpec((1, tm, 256), lambda bb, i: (bb, i, 0)),
            pl.BlockSpec((1, tm, 256), lambda bb, i: (bb, i, 0)),
            pl.BlockSpec((1, tm, 512), lambda bb, i: (bb, i, 0)),
            pl.BlockSpec((1, 6, d), lambda bb, i: (bb, 0, 0)),
            pl.BlockSpec((1, d), const2),
            pl.BlockSpec((256, d), const2, pipeline_mode=once),
            pl.BlockSpec((256, d), const2, pipeline_mode=once),
            pl.BlockSpec((512, d), const2, pipeline_mode=once),
            pl.BlockSpec((d, dff), const2, pipeline_mode=once),
            pl.BlockSpec((d, dff), const2, pipeline_mode=once),
            pl.BlockSpec((dff, d), const2, pipeline_mode=once),
        ],
        out_specs=pl.BlockSpec((1, tm, d), lambda bb, i: (bb, i, 0)),
        out_shape=jax.ShapeDtypeStruct((b, l, d), F32),
        compiler_params=_cparams(("parallel", "arbitrary")),
        name="merge_ffn",
    )(x, oa, ob, oc, mod_l, ln, woa, wob, woc, wg, wu, wd)


def _rope_tables(l):
    inv = 1.0 / (ROPE_THETA ** (jnp.arange(0, HEAD_DIM, 2, dtype=F32) / HEAD_DIM))
    ang = jnp.arange(l, dtype=F32)[:, None] * inv[None, :]
    cos, sin = jnp.cos(ang), jnp.sin(ang)
    cos_n = jnp.tile(cos, (1, 4))
    sin_n = jnp.concatenate([-sin, sin, -sin, sin], axis=1)
    return cos.T, sin.T, cos_n, sin_n


def _split_w_in(w_in):
    widths = (256, 64, 64, 256, 64, 4, 256, 256, 256, 512, 128, 128)
    offs = [0]
    for w in widths:
        offs.append(offs[-1] + w)
    qa, ka, va, qi, ki, wi, qb, kb, vb, qc, kc, vc = [w_in[:, :, offs[k]:offs[k + 1]]
                                                      for k in range(len(widths))]
    idx_scale = float((N_IDX_HEADS * HEAD_DIM) ** -0.5)
    att_scale = float(HEAD_DIM ** -0.5)
    pad = jnp.zeros(wi.shape[:2] + (R_TOTAL - R_W - N_IDX_HEADS,), w_in.dtype)
    wt = jnp.concatenate([qa, qi, qb * (att_scale * LOG2E), qc, va, vb, vc, wi * idx_scale, pad],
                         axis=2)
    wt = jnp.swapaxes(wt, 1, 2).astype(BF16)
    wn = jnp.concatenate([ka, ki, kb, kc], axis=2).astype(BF16)
    return wt, wn


@jax.jit
def kernel(x, c, ln1, ln2, w_mod, b_mod, w_in, qn_a, kn_a, qn_c, kn_c, sinks, g_out, w_o,
           w_gate, w_up, w_down):
    depth = w_in.shape[0]
    b, l, d = x.shape
    assert l % TOK_TILE == 0 and l >= ATT_TILE + WINDOW
    topk = min(TOPK_MAX, l // 4)
    att_scale = float(HEAD_DIM ** -0.5)

    tabs = _rope_tables(l)
    mod = _modulation(c, w_mod, b_mod)
    wt, wn = _split_w_in(w_in)
    w_o_b = w_o.astype(BF16)
    w_g_b, w_u_b, w_d_b = w_gate.astype(BF16), w_up.astype(BF16), w_down.astype(BF16)
    ones = jnp.ones((HEAD_DIM,), F32)

    for li in range(depth):
        gains = ((qn_a[li] * (att_scale * LOG2E)).reshape(HEAD_DIM, 1),
                 (qn_c[li] * (att_scale * LOG2E)).reshape(HEAD_DIM, 1),
                 jnp.concatenate([kn_a[li], ones]).reshape(1, LANES),
                 jnp.concatenate([kn_c[li], kn_c[li]]).reshape(1, LANES))
        (qa, qi, qb, qc, w, vat, vbt, vct, kaki, kb, kc) = _inproj(
            x, mod[li], ln1[li].reshape(1, d), wt[li], wn[li], tabs, gains)
        g = g_out[li].reshape(1, -1)
        oa = _dsa(qi, qa, w, kaki, vat, g[:, 0:256], topk)
        ob, oc = _sb_swa(sinks[li], qb, kb, vbt, g[:, 256:512], qc, kc, vct, g[:, 512:1024])
        x = _merge_ffn(x, oa, ob, oc, mod[li], ln2[li].reshape(1, d),
                       w_o_b[li, 0:256], w_o_b[li, 256:512], w_o_b[li, 512:1024],
                       w_g_b[li], w_u_b[li], w_d_b[li])
    return x
```
